```python
import math
import jax, jax.numpy as jnp
from jax import lax
import numpy as np

D_MODEL = 4096
BATCH = 4
SEQ = 4096
DEPTH = 1

HEAD_DIM = 128
MOBA_HEADS = D_MODEL // (2 * HEAD_DIM)
MOBA_WIDTH = MOBA_HEADS * HEAD_DIM
DIFF_HEADS = D_MODEL // (4 * HEAD_DIM)
DIFF_V_DIM = 2 * HEAD_DIM
DIFF_WIDTH = DIFF_HEADS * DIFF_V_DIM
MIX_WIDTH = MOBA_WIDTH + DIFF_WIDTH
IN_WIDTH = 3 * MOBA_WIDTH + 3 * DIFF_HEADS * 2 * HEAD_DIM
MOBA_BLOCK = 256
MOBA_TOPK = 3
MOBA_Q_CHUNK = 64
DIFF_Q_BLOCK = 128
ROPE_THETA = 10000.0
D_FF = ((8 * D_MODEL // 3 + 255) // 256) * 256
CONV_WIDTH = 3
RMS_EPS = 1e-6
SUBLN_EPS = 1e-5
NEG_INF = -1e30

kernel_name = "hybrid_moba_diffattn_convffn"


def lambda_init(layer_idx):
    return 0.8 - 0.6 * math.exp(-0.3 * layer_idx)


def rms_norm(x, w, eps):
    xf = x.astype(jnp.float32)
    y = xf * lax.rsqrt(jnp.mean(xf * xf, axis=-1, keepdims=True) + eps)
    return (y * w.astype(jnp.float32)).astype(x.dtype)


def rope_tables(seq_len):
    inv = 1.0 / (ROPE_THETA ** (jnp.arange(0, HEAD_DIM, 2, dtype=jnp.float32) / HEAD_DIM))
    ang = jnp.arange(seq_len, dtype=jnp.float32)[:, None] * inv[None, :]
    return jnp.cos(ang), jnp.sin(ang)


def apply_rope(x, cos, sin):
    x1, x2 = jnp.split(x, 2, axis=-1)
    c = cos.astype(x.dtype)
    s = sin.astype(x.dtype)
    return jnp.concatenate([x1 * c - x2 * s, x1 * s + x2 * c], axis=-1)


def moba_attention(q, k, v):
    B, H, S, Dh = q.shape
    nb = -(-S // MOBA_BLOCK)
    s_pad = nb * MOBA_BLOCK
    pad = ((0, 0), (0, 0), (0, s_pad - S), (0, 0))
    q = jnp.pad(q, pad)
    k = jnp.pad(k, pad)
    v = jnp.pad(v, pad)
    kb = k.reshape(B, H, nb, MOBA_BLOCK, Dh)
    vb = v.reshape(B, H, nb, MOBA_BLOCK, Dh)
    k_mean = jnp.mean(kb.astype(jnp.float32), axis=3).astype(k.dtype)
    topk = min(MOBA_TOPK, nb)
    n_chunks = s_pad // MOBA_Q_CHUNK
    chunks_per_block = MOBA_BLOCK // MOBA_Q_CHUNK
    scale = Dh ** -0.5
    blk_ids = jnp.arange(nb)

    def one_batch(args):
        qb_, kb_, vb_, km_ = args

        def one_chunk(c):
            start = c * MOBA_Q_CHUNK
            blk = c // chunks_per_block
            qc = lax.dynamic_slice_in_dim(qb_, start, MOBA_Q_CHUNK, axis=1)
            gate = jnp.einsum('hqd,hnd->hqn', qc, km_).astype(jnp.float32)
            gate = jnp.where((blk_ids < blk)[None, None, :], gate, -jnp.inf)
            _, sel = lax.top_k(gate, topk)
            sel_ok = sel < blk
            k_sel = jax.vmap(lambda kk, ii: kk[ii])(kb_, sel)
            v_sel = jax.vmap(lambda vv, ii: vv[ii])(vb_, sel)
            s_sel = jnp.einsum('hqd,hqjkd->hqjk', qc, k_sel).astype(jnp.float32) * scale
            s_sel = jnp.where(sel_ok[..., None], s_sel, NEG_INF).reshape(H, MOBA_Q_CHUNK, topk * MOBA_BLOCK)
            k_own = lax.dynamic_index_in_dim(kb_, blk, axis=1, keepdims=False)
            v_own = lax.dynamic_index_in_dim(vb_, blk, axis=1, keepdims=False)
            s_own = jnp.einsum('hqd,hkd->hqk', qc, k_own).astype(jnp.float32) * scale
            q_pos = start + jnp.arange(MOBA_Q_CHUNK)
            k_pos = blk * MOBA_BLOCK + jnp.arange(MOBA_BLOCK)
            s_own = jnp.where(k_pos[None, None, :] <= q_pos[None, :, None], s_own, NEG_INF)
            p = jax.nn.softmax(jnp.concatenate([s_sel, s_own], axis=-1), axis=-1).astype(v.dtype)
            p_sel = p[..., :topk * MOBA_BLOCK].reshape(H, MOBA_Q_CHUNK, topk, MOBA_BLOCK)
            p_own = p[..., topk * MOBA_BLOCK:]
            return (jnp.einsum('hqjk,hqjkd->hqd', p_sel, v_sel)
                    + jnp.einsum('hqk,hkd->hqd', p_own, v_own))

        outs = lax.map(one_chunk, jnp.arange(n_chunks))
        return outs.transpose(1, 0, 2, 3).reshape(H, s_pad, Dh)

    out = lax.map(one_batch, (q, kb, vb, k_mean))
    return out[:, :, :S]


def diff_attention(q, k, v, lam):
    B, H, _, S, Dh = q.shape
    n_blocks = S // DIFF_Q_BLOCK
    scale = Dh ** -0.5
    k_pos = jnp.arange(S)

    def one_block(i):
        qi = lax.dynamic_slice_in_dim(q, i * DIFF_Q_BLOCK, DIFF_Q_BLOCK, axis=3)
        s = jnp.einsum('bhcqd,bhckd->bhcqk', qi, k).astype(jnp.float32) * scale
        q_pos = i * DIFF_Q_BLOCK + jnp.arange(DIFF_Q_BLOCK)
        s = jnp.where(k_pos[None, :] <= q_pos[:, None], s, NEG_INF)
        p = jax.nn.softmax(s, axis=-1)
        a = p[:, :, 0] - lam * p[:, :, 1]
        return jnp.einsum('bhqk,bhkd->bhqd', a.astype(v.dtype), v)

    out = lax.map(one_block, jnp.arange(n_blocks))
    return out.transpose(1, 2, 0, 3, 4).reshape(B, H, S, 2 * Dh)


def conv_gated_mlp(h, w_up, conv_w, conv_b, w_down):
    u = h @ w_up
    C = u.shape[-1]
    u = lax.conv_general_dilated(
        u, conv_w[:, None, :].astype(u.dtype), window_strides=(1,),
        padding=[(CONV_WIDTH - 1, 0)], dimension_numbers=('NWC', 'WIO', 'NWC'),
        feature_group_count=C) + conv_b
    gate, up = jnp.split(u, 2, axis=-1)
    return (jax.nn.silu(gate) * up) @ w_down


def setup_inputs(seed: int = 0) -> dict:
    key = jax.random.key(seed)
    ks = jax.random.split(key, 16)
    f32 = jnp.float32
    nrm = lambda k, shape, s: jax.random.normal(k, shape, f32) * s
    return {
        "x": nrm(ks[0], (BATCH, SEQ, D_MODEL), 1.0),
        "attn_norm_w": 1.0 + nrm(ks[1], (DEPTH, D_MODEL), 0.02),
        "w_in": nrm(ks[2], (DEPTH, D_MODEL, IN_WIDTH), D_MODEL ** -0.5),
        "diff_lambda_q1": nrm(ks[3], (DEPTH, HEAD_DIM), 0.1),
        "diff_lambda_k1": nrm(ks[4], (DEPTH, HEAD_DIM), 0.1),
        "diff_lambda_q2": nrm(ks[5], (DEPTH, HEAD_DIM), 0.1),
        "diff_lambda_k2": nrm(ks[6], (DEPTH, HEAD_DIM), 0.1),
        "diff_subln_w": 1.0 + nrm(ks[7], (DEPTH, DIFF_V_DIM), 0.02),
        "w_out": nrm(ks[8], (DEPTH, MIX_WIDTH, D_MODEL), MIX_WIDTH ** -0.5),
        "ffn_norm_w": 1.0 + nrm(ks[9], (DEPTH, D_MODEL), 0.02),
        "w_up": nrm(ks[10], (DEPTH, D_MODEL, 2 * D_FF), D_MODEL ** -0.5),
        "conv_w": nrm(ks[11], (DEPTH, CONV_WIDTH, 2 * D_FF), CONV_WIDTH ** -0.5),
        "conv_b": nrm(ks[12], (DEPTH, 2 * D_FF), 0.01),
        "w_down": nrm(ks[13], (DEPTH, D_FF, D_MODEL), D_FF ** -0.5),
        "final_norm_w": 1.0 + nrm(ks[14], (D_MODEL,), 0.02),
    }


def reference(x, attn_norm_w, w_in, diff_lambda_q1, diff_lambda_k1, diff_lambda_q2,
              diff_lambda_k2, diff_subln_w, w_out, ffn_norm_w, w_up, conv_w, conv_b,
              w_down, final_norm_w):
    B, S, _ = x.shape
    cos, sin = rope_tables(S)
    for l in range(DEPTH):
        lam_init = lambda_init(l)
        h = rms_norm(x, attn_norm_w[l], RMS_EPS)
        proj = h @ w_in[l]
        dqk = DIFF_HEADS * 2 * HEAD_DIM
        o1 = MOBA_WIDTH
        o2 = 2 * MOBA_WIDTH
        o3 = 3 * MOBA_WIDTH
        mq, mk, mv = proj[..., :o1], proj[..., o1:o2], proj[..., o2:o3]
        dq = proj[..., o3:o3 + dqk]
        dk = proj[..., o3 + dqk:o3 + 2 * dqk]
        dv = proj[..., o3 + 2 * dqk:]
        to_heads = lambda t: t.reshape(B, S, MOBA_HEADS, HEAD_DIM).transpose(0, 2, 1, 3)
        mq = apply_rope(to_heads(mq), cos, sin)
        mk = apply_rope(to_heads(mk), cos, sin)
        mv = to_heads(mv)
        moba_out = moba_attention(mq, mk, mv)
        moba_out = moba_out.transpose(0, 2, 1, 3).reshape(B, S, MOBA_WIDTH)
        to_diff = lambda t: t.reshape(B, S, DIFF_HEADS, 2, HEAD_DIM).transpose(0, 2, 3, 1, 4)
        dq = apply_rope(to_diff(dq), cos, sin)
        dk = apply_rope(to_diff(dk), cos, sin)
        dv = dv.reshape(B, S, DIFF_HEADS, DIFF_V_DIM).transpose(0, 2, 1, 3)
        f32 = jnp.float32
        lam = (jnp.exp(jnp.sum(diff_lambda_q1[l].astype(f32) * diff_lambda_k1[l].astype(f32)))
               - jnp.exp(jnp.sum(diff_lambda_q2[l].astype(f32) * diff_lambda_k2[l].astype(f32)))
               + lam_init)
        diff_out = diff_attention(dq, dk, dv, lam)
        diff_out = rms_norm(diff_out, diff_subln_w[l], SUBLN_EPS) * (1.0 - lam_init)
        diff_out = diff_out.transpose(0, 2, 1, 3).reshape(B, S, DIFF_WIDTH)
        mixed = jnp.concatenate([moba_out, diff_out], axis=-1)
        x = x + mixed @ w_out[l]
        h = rms_norm(x, ffn_norm_w[l], RMS_EPS)
        x = x + conv_gated_mlp(h, w_up[l], conv_w[l], conv_b[l], w_down[l])
    return rms_norm(x, final_norm_w, RMS_EPS)
```

```python
import functools
import math

import jax
import jax.numpy as jnp
from jax import lax
from jax.experimental import pallas as pl
from jax.experimental.pallas import tpu as pltpu

HEAD_DIM = 128
MOBA_BLOCK = 256
MOBA_TOPK = 3
ROPE_THETA = 10000.0
CONV_WIDTH = 3
RMS_EPS = 1e-6
SUBLN_EPS = 1e-5
NEG_INF = -1e30
DIFF_LAMBDA_INIT = 0.8 - 0.6 * math.exp(-0.3 * 0)

V7X_VMEM_LIMIT_BYTES = 56 * 1024 * 1024
SUBLANES_F32 = 8
SUBLANES_BF16 = 16

F32 = jnp.float32
BF16 = jnp.bfloat16
_NT_DIMS = (((1,), (1,)), ((), ()))


def _cparams(*semantics):
    return pltpu.CompilerParams(dimension_semantics=semantics,
                                vmem_limit_bytes=V7X_VMEM_LIMIT_BYTES)


def _pick_tile(n, target, quantum):
    t = min(n, target) // quantum * quantum
    while t > quantum and n % t:
        t -= quantum
    assert t >= quantum and n % t == 0, (n, target, quantum)
    return t


def _rmsnorm_kernel(x_ref, w_ref, o_ref, *, eps):
    x = x_ref[...]
    y = x * lax.rsqrt(jnp.mean(x * x, axis=-1, keepdims=True) + eps)
    o_ref[...] = (y * w_ref[...]).astype(o_ref.dtype)


def _rmsnorm(x, w, eps, out_dtype):
    m, d = x.shape
    tm = _pick_tile(m, 256, SUBLANES_BF16)
    return pl.pallas_call(
        functools.partial(_rmsnorm_kernel, eps=eps),
        grid=(m // tm,),
        in_specs=[pl.BlockSpec((tm, d), lambda i: (i, 0)),
                  pl.BlockSpec((1, d), lambda i: (0, 0))],
        out_specs=pl.BlockSpec((tm, d), lambda i: (i, 0)),
        out_shape=jax.ShapeDtypeStruct((m, d), out_dtype),
        compiler_params=_cparams("arbitrary"),
        name="rmsnorm",
    )(x, w.reshape(1, d).astype(F32))


def _inproj_kernel(h_ref, w_ref, cs_ref, o_ref, *, tiles_per_group):
    j = pl.program_id(1)
    acc = jnp.dot(h_ref[...], w_ref[...], preferred_element_type=F32)
    group = (j // tiles_per_group) % 3

    @pl.when(group == 2)
    def _():
        o_ref[...] = acc.astype(o_ref.dtype)

    @pl.when(group != 2)
    def _():
        cos = cs_ref[0, 0]
        sin = cs_ref[0, 1]
        for hh in range(acc.shape[1] // HEAD_DIM):
            cols = slice(hh * HEAD_DIM, (hh + 1) * HEAD_DIM)
            a = acc[:, cols]
            rot = pltpu.roll(a, HEAD_DIM // 2, 1)
            o_ref[:, cols] = (a * cos + rot * sin).astype(o_ref.dtype)


def _inproj(h, w, tables, seq, group_width):
    m, d = h.shape
    n = w.shape[1]
    tm = _pick_tile(seq, 1024, SUBLANES_BF16)
    tn = _pick_tile(group_width, 512, HEAD_DIM)
    tiles_per_group = group_width // tn
    seq_tiles = seq // tm

    def table_map(i, j):
        return (jnp.minimum((j // tiles_per_group) % 3, 1), 0, i % seq_tiles, 0)

    return pl.pallas_call(
        functools.partial(_inproj_kernel, tiles_per_group=tiles_per_group),
        grid=(m // tm, n // tn),
        in_specs=[pl.BlockSpec((tm, d), lambda i, j: (i, 0)),
                  pl.BlockSpec((d, tn), lambda i, j: (0, j)),
                  pl.BlockSpec((1, 2, tm, HEAD_DIM), table_map)],
        out_specs=pl.BlockSpec((tm, tn), lambda i, j: (i, j)),
        out_shape=jax.ShapeDtypeStruct((m, n), BF16),
        compiler_params=_cparams("arbitrary", "arbitrary"),
        name="inproj_rope",
    )(h, w, tables)


def _rope_tables(seq, q_scale):
    inv = 1.0 / (ROPE_THETA ** (jnp.arange(0, HEAD_DIM, 2, dtype=F32) / HEAD_DIM))
    ang = jnp.arange(seq, dtype=F32)[:, None] * inv[None, :]
    cos, sin = jnp.cos(ang), jnp.sin(ang)
    cos_full = jnp.concatenate([cos, cos], axis=-1)
    sin_signed = jnp.concatenate([-sin, sin], axis=-1)
    k_tab = jnp.stack([cos_full, sin_signed])
    return jnp.stack([k_tab * q_scale, k_tab])


def _softmax_step(s_t, m_prev, l_prev, acc_prev, v_t):
    m_new = jnp.maximum(m_prev, jnp.max(s_t, axis=0, keepdims=True))
    alpha = jnp.exp(m_prev - m_new)
    p = jnp.exp(s_t - m_new)
    l_new = alpha * l_prev + jnp.sum(p, axis=0, keepdims=True)
    acc_new = alpha * acc_prev + jnp.dot(v_t, p.astype(BF16), preferred_element_type=F32)
    return m_new, l_new, acc_new


def _own_block_scores(k_own, q):
    blk = k_own.shape[0]
    s_t = lax.dot_general(k_own, q, _NT_DIMS, preferred_element_type=F32)
    k_pos = lax.broadcasted_iota(jnp.int32, (blk, blk), 0)
    q_pos = lax.broadcasted_iota(jnp.int32, (blk, blk), 1)
    return jnp.where(k_pos <= q_pos, s_t, NEG_INF)


def _moba_kernel(q_ref, k_ref, v_ref, o_ref, vt_sc, kmean_sc, bias_sc, *, n_blocks, topk):
    i = pl.program_id(2)
    blk = MOBA_BLOCK
    nbp = bias_sc.shape[0]
    seq = k_ref.shape[0]

    @pl.when(i == 0)
    def _():
        for n in range(n_blocks):
            vt_sc[n] = v_ref[n * blk:(n + 1) * blk, :].astype(F32).T.astype(BF16)
        row = lax.broadcasted_iota(jnp.int32, (nbp, seq), 0)
        col = lax.broadcasted_iota(jnp.int32, (nbp, seq), 1)
        lo_edge = row * blk
        avg = jnp.where((col >= lo_edge) & (col < lo_edge + blk), 1.0 / blk, 0.0).astype(BF16)
        kmean = jnp.dot(avg, k_ref[...], preferred_element_type=F32)
        hi = kmean.astype(BF16)
        kmean_sc[0:nbp, :] = hi
        kmean_sc[nbp:2 * nbp, :] = (kmean - hi.astype(F32)).astype(BF16)

    q = q_ref[...]

    g2 = lax.dot_general(kmean_sc[...], q, _NT_DIMS, preferred_element_type=F32)
    g = g2[0:nbp, :] + g2[nbp:2 * nbp, :]
    row = lax.broadcasted_iota(jnp.int32, g.shape, 0)
    valid = row < i
    for n in range(n_blocks):
        gn = g[n:n + 1, :]
        beats = valid & ((g > gn) | ((g == gn) & (row < n)))
        rank = jnp.sum(beats.astype(F32), axis=0, keepdims=True)
        bias_sc[n:n + 1, :] = jnp.where(rank < topk, 0.0, NEG_INF)

    start = pl.multiple_of(i * blk, blk)
    s_t = _own_block_scores(k_ref[pl.ds(start, blk), :], q)
    m0 = jnp.max(s_t, axis=0, keepdims=True)
    p = jnp.exp(s_t - m0)
    l0 = jnp.sum(p, axis=0, keepdims=True)
    acc0 = jnp.dot(vt_sc[i], p.astype(BF16), preferred_element_type=F32)

    def body(j, carry):
        kj = k_ref[pl.ds(pl.multiple_of(j * blk, blk), blk), :]
        s_j = lax.dot_general(kj, q, _NT_DIMS, preferred_element_type=F32)
        s_j = s_j + bias_sc[pl.ds(j, 1), :]
        return _softmax_step(s_j, *carry, vt_sc[j])

    _, l, acc = lax.fori_loop(0, i, body, (m0, l0, acc0))
    o_ref[...] = (acc / l).T.astype(o_ref.dtype)


def _moba_attention(qkv, batch, seq, heads):
    blk = MOBA_BLOCK
    assert seq % blk == 0
    n_blocks = seq // blk
    nbp = -(-n_blocks // SUBLANES_BF16) * SUBLANES_BF16
    topk = min(MOBA_TOPK, n_blocks)
    return pl.pallas_call(
        functools.partial(_moba_kernel, n_blocks=n_blocks, topk=topk),
        grid=(batch, heads, n_blocks),
        in_specs=[pl.BlockSpec((blk, HEAD_DIM), lambda b, h, i: (b * n_blocks + i, h)),
                  pl.BlockSpec((seq, HEAD_DIM), lambda b, h, i: (b, heads + h)),
                  pl.BlockSpec((seq, HEAD_DIM), lambda b, h, i: (b, 2 * heads + h))],
        out_specs=pl.BlockSpec((blk, HEAD_DIM), lambda b, h, i: (b * n_blocks + i, h)),
        out_shape=jax.ShapeDtypeStruct((batch * seq, heads * HEAD_DIM), BF16),
        scratch_shapes=[pltpu.VMEM((n_blocks, HEAD_DIM, blk), BF16),
                        pltpu.VMEM((2 * nbp, HEAD_DIM), BF16),
                        pltpu.VMEM((nbp, blk), F32)],
        compiler_params=_cparams("arbitrary", "arbitrary", "arbitrary"),
        name="moba_attention",
    )(qkv, qkv, qkv)


def _diff_kernel(q_ref, k_ref, v_ref, lq1_ref, lk1_ref, lq2_ref, lk2_ref, subln_ref, o_ref,
                 vt_sc, *, n_blocks, lam_init):
    i = pl.program_id(2)
    blk = q_ref.shape[0]

    @pl.when(i == 0)
    def _():
        for n in range(n_blocks):
            vt_sc[n] = v_ref[n * blk:(n + 1) * blk, :].astype(F32).T.astype(BF16)

    lam = (jnp.exp(jnp.sum(lq1_ref[...] * lk1_ref[...], axis=-1, keepdims=True))
           - jnp.exp(jnp.sum(lq2_ref[...] * lk2_ref[...], axis=-1, keepdims=True))
           + lam_init)

    start = pl.multiple_of(i * blk, blk)
    outs = []
    for c in range(2):
        cols = slice(c * HEAD_DIM, (c + 1) * HEAD_DIM)
        q = q_ref[:, cols]
        s_t = _own_block_scores(k_ref[pl.ds(start, blk), cols], q)
        m0 = jnp.max(s_t, axis=0, keepdims=True)
        p = jnp.exp(s_t - m0)
        l0 = jnp.sum(p, axis=0, keepdims=True)
        acc0 = jnp.dot(vt_sc[i], p.astype(BF16), preferred_element_type=F32)

        def body(j, carry, q=q, cols=cols):
            kj = k_ref[pl.ds(pl.multiple_of(j * blk, blk), blk), cols]
            s_j = lax.dot_general(kj, q, _NT_DIMS, preferred_element_type=F32)
            return _softmax_step(s_j, *carry, vt_sc[j])

        _, l, acc = lax.fori_loop(0, i, body, (m0, l0, acc0))
        outs.append(acc / l)

    d = outs[0] - lam * outs[1]
    y = d * lax.rsqrt(jnp.mean(d * d, axis=0, keepdims=True) + SUBLN_EPS)
    y = (y * subln_ref[...]) * (1.0 - lam_init)
    o_ref[...] = y.T.astype(o_ref.dtype)


def _diff_attention(qkv, lq1, lk1, lq2, lk2, subln_w, batch, seq, heads, col0):
    blk = 256
    vdim = 2 * HEAD_DIM
    assert seq % blk == 0 and col0 % vdim == 0
    n_blocks = seq // blk
    c0 = col0 // vdim
    vec = pl.BlockSpec((1, HEAD_DIM), lambda b, h, i: (0, 0))
    as_row = lambda t: t.reshape(1, HEAD_DIM).astype(F32)
    return pl.pallas_call(
        functools.partial(_diff_kernel, n_blocks=n_blocks, lam_init=DIFF_LAMBDA_INIT),
        grid=(batch, heads, n_blocks),
        in_specs=[pl.BlockSpec((blk, vdim), lambda b, h, i: (b * n_blocks + i, c0 + h)),
                  pl.BlockSpec((seq, vdim), lambda b, h, i: (b, c0 + heads + h)),
                  pl.BlockSpec((seq, vdim), lambda b, h, i: (b, c0 + 2 * heads + h)),
                  vec, vec, vec, vec,
                  pl.BlockSpec((vdim, 1), lambda b, h, i: (0, 0))],
        out_specs=pl.BlockSpec((blk, vdim), lambda b, h, i: (b * n_blocks + i, h)),
        out_shape=jax.ShapeDtypeStruct((batch * seq, heads * vdim), BF16),
        scratch_shapes=[pltpu.VMEM((n_blocks, vdim, blk), BF16)],
        compiler_params=_cparams("arbitrary", "arbitrary", "arbitrary"),
        name="diff_attention",
    )(qkv, qkv, qkv, as_row(lq1), as_row(lk1), as_row(lq2), as_row(lk2),
      subln_w.reshape(vdim, 1).astype(F32))


def _outproj_kernel(a_ref, b_ref, wa_ref, wb_ref, x_ref, o_ref):
    acc = jnp.dot(a_ref[...], wa_ref[...], preferred_element_type=F32)
    acc += jnp.dot(b_ref[...], wb_ref[...], preferred_element_type=F32)
    o_ref[...] = x_ref[...] + acc


def _outproj(a, b, wa, wb, x):
    m, ka = a.shape
    kb = b.shape[1]
    n = wa.shape[1]
    tm = _pick_tile(m, 1024, SUBLANES_BF16)
    tn = _pick_tile(n, 512, HEAD_DIM)
    return pl.pallas_call(
        _outproj_kernel,
        grid=(m // tm, n // tn),
        in_specs=[pl.BlockSpec((tm, ka), lambda i, j: (i, 0)),
                  pl.BlockSpec((tm, kb), lambda i, j: (i, 0)),
                  pl.BlockSpec((ka, tn), lambda i, j: (0, j)),
                  pl.BlockSpec((kb, tn), lambda i, j: (0, j)),
                  pl.BlockSpec((tm, tn), lambda i, j: (i, j))],
        out_specs=pl.BlockSpec((tm, tn), lambda i, j: (i, j)),
        out_shape=jax.ShapeDtypeStruct((m, n), F32),
        compiler_params=_cparams("arbitrary", "arbitrary"),
        name="outproj_residual",
    )(a, b, wa, wb, x)


_CONV_CHUNK_ROWS = 128


def _up_kernel(h_ref, wg_ref, wu_ref, pg_ref, pu_ref, o_ref,
               g_sc, u_sc, cg_sc, cu_sc, *, seq_tiles):
    i = pl.program_id(0)
    j = pl.program_id(1)
    tm = h_ref.shape[0]
    halo = SUBLANES_F32
    seq_start = i % seq_tiles == 0

    h = h_ref[...]
    for sc, w_ref, carry in ((g_sc, wg_ref, cg_sc), (u_sc, wu_ref, cu_sc)):
        @pl.when(seq_start)
        def _(sc=sc):
            sc[0:halo, :] = jnp.zeros((halo, sc.shape[1]), F32)

        @pl.when(jnp.logical_not(seq_start))
        def _(sc=sc, carry=carry):
            sc[0:halo, :] = carry[j]

        sc[halo:halo + tm, :] = jnp.dot(h, w_ref[...], preferred_element_type=F32)
        carry[j] = sc[tm:tm + halo, :]

    def conv(sc, p_ref, r0):
        rows = _CONV_CHUNK_ROWS
        out = p_ref[CONV_WIDTH:CONV_WIDTH + 1, :]
        for tap in range(CONV_WIDTH):
            off = r0 + halo - (CONV_WIDTH - 1) + tap
            out = out + sc[off:off + rows, :] * p_ref[tap:tap + 1, :]
        return out

    for r0 in range(0, tm, _CONV_CHUNK_ROWS):
        g = conv(g_sc, pg_ref, r0)
        u = conv(u_sc, pu_ref, r0)
        o_ref[r0:r0 + _CONV_CHUNK_ROWS, :] = (g * jax.nn.sigmoid(g) * u).astype(o_ref.dtype)


def _up_conv_gate(h, wg, wu, pg, pu, seq):
    m, d = h.shape
    n = wg.shape[1]
    tm = _pick_tile(seq, 1024, _CONV_CHUNK_ROWS)
    tn = _pick_tile(n, 512, HEAD_DIM)
    halo = SUBLANES_F32
    col = lambda i, j: (0, j)
    return pl.pallas_call(
        functools.partial(_up_kernel, seq_tiles=seq // tm),
        grid=(m // tm, n // tn),
        in_specs=[pl.BlockSpec((tm, d), lambda i, j: (i, 0)),
                  pl.BlockSpec((d, tn), col),
                  pl.BlockSpec((d, tn), col),
                  pl.BlockSpec((CONV_WIDTH + 1, tn), col),
                  pl.BlockSpec((CONV_WIDTH + 1, tn), col)],
        out_specs=pl.BlockSpec((tm, tn), lambda i, j: (i, j)),
        out_shape=jax.ShapeDtypeStruct((m, n), BF16),
        scratch_shapes=[pltpu.VMEM((tm + halo, tn), F32),
                        pltpu.VMEM((tm + halo, tn), F32),
                        pltpu.VMEM((n // tn, halo, tn), F32),
                        pltpu.VMEM((n // tn, halo, tn), F32)],
        compiler_params=_cparams("arbitrary", "arbitrary"),
        name="up_conv_gate",
    )(h, wg, wu, pg, pu)


def _down_kernel(a_ref, w_ref, x_ref, o_ref):
    part = jnp.dot(a_ref[...], w_ref[...], preferred_element_type=F32)

    @pl.when(pl.program_id(2) == 0)
    def _():
        o_ref[...] = x_ref[...] + part

    @pl.when(pl.program_id(2) != 0)
    def _():
        o_ref[...] += part


def _down_residual(a, w, x):
    m, k = a.shape
    n = w.shape[1]
    tm = _pick_tile(m, 1024, SUBLANES_BF16)
    tn = _pick_tile(n, 1024, HEAD_DIM)
    tk = _pick_tile(k, 1024, HEAD_DIM)
    return pl.pallas_call(
        _down_kernel,
        grid=(m // tm, n // tn, k // tk),
        in_specs=[pl.BlockSpec((tm, tk), lambda i, j, kk: (i, kk)),
                  pl.BlockSpec((tk, tn), lambda i, j, kk: (kk, j)),
                  pl.BlockSpec((tm, tn), lambda i, j, kk: (i, j))],
        out_specs=pl.BlockSpec((tm, tn), lambda i, j, kk: (i, j)),
        out_shape=jax.ShapeDtypeStruct((m, n), F32),
        compiler_params=_cparams("arbitrary", "arbitrary", "arbitrary"),
        name="down_residual",
    )(a, w, x)


def _pad_cols(t, n):
    return jnp.pad(t, ((0, 0), (0, n - t.shape[1])))


def kernel(x, attn_norm_w, w_in, diff_lambda_q1, diff_lambda_k1, diff_lambda_q2, diff_lambda_k2, diff_subln_w, w_out, ffn_norm_w, w_up, conv_w, conv_b, w_down, final_norm_w):
    batch, seq, d_model = x.shape
    depth = w_in.shape[0]
    assert depth == 1, "lambda_init is folded for a single layer"
    moba_heads = d_model // (2 * HEAD_DIM)
    diff_heads = d_model // (4 * HEAD_DIM)
    moba_width = moba_heads * HEAD_DIM
    d_ff = w_down.shape[1]
    ff_quantum = 1024 if d_ff >= 1024 else HEAD_DIM
    d_ff_pad = -(-d_ff // ff_quantum) * ff_quantum

    xf = x.reshape(batch * seq, d_model)
    tables = _rope_tables(seq, HEAD_DIM ** -0.5)
    for l in range(depth):
        h = _rmsnorm(xf, attn_norm_w[l], RMS_EPS, BF16)
        qkv = _inproj(h, w_in[l].astype(BF16), tables, seq, moba_width)
        moba = _moba_attention(qkv, batch, seq, moba_heads)
        diff = _diff_attention(qkv, diff_lambda_q1[l], diff_lambda_k1[l], diff_lambda_q2[l],
                               diff_lambda_k2[l], diff_subln_w[l], batch, seq, diff_heads,
                               3 * moba_width)
        w_o = w_out[l].astype(BF16)
        xf = _outproj(moba, diff, w_o[:moba_width], w_o[moba_width:], xf)

        h = _rmsnorm(xf, ffn_norm_w[l], RMS_EPS, BF16)
        conv_params = jnp.concatenate([conv_w[l], conv_b[l][None, :]], axis=0).astype(F32)
        wg = _pad_cols(w_up[l][:, :d_ff].astype(BF16), d_ff_pad)
        wu = _pad_cols(w_up[l][:, d_ff:].astype(BF16), d_ff_pad)
        pg = _pad_cols(conv_params[:, :d_ff], d_ff_pad)
        pu = _pad_cols(conv_params[:, d_ff:], d_ff_pad)
        act = _up_conv_gate(h, wg, wu, pg, pu, seq)
        w_d = jnp.pad(w_down[l].astype(BF16), ((0, d_ff_pad - d_ff), (0, 0)))
        xf = _down_residual(act, w_d, xf)
    out = _rmsnorm(xf, final_norm_w, RMS_EPS, x.dtype)
    return out.reshape(batch, seq, d_model)
```

```python
import functools
import math

import jax
import jax.numpy as jnp
from jax import lax
from jax.experimental import pallas as pl
from jax.experimental.pallas import tpu as pltpu

HEAD_DIM = 128
MOBA_BLOCK = 256
MOBA_TOPK = 3
ROPE_THETA = 10000.0
CONV_WIDTH = 3
RMS_EPS = 1e-6
SUBLN_EPS = 1e-5
NEG_INF = -1e30
DIFF_LAMBDA_INIT = 0.8 - 0.6 * math.exp(-0.3 * 0)

V7X_VMEM_LIMIT_BYTES = 56 * 1024 * 1024
SUBLANES_F32 = 8
SUBLANES_BF16 = 16

F32 = jnp.float32
BF16 = jnp.bfloat16
_NT_DIMS = (((1,), (1,)), ((), ()))


def _cparams(*semantics):
    return pltpu.CompilerParams(dimension_semantics=semantics,
                                vmem_limit_bytes=V7X_VMEM_LIMIT_BYTES)


def _pick_tile(n, target, quantum):
    t = min(n, target) // quantum * quantum
    while t > quantum and n % t:
        t -= quantum
    assert t >= quantum and n % t == 0, (n, target, quantum)
    return t


def _rmsnorm_kernel(x_ref, w_ref, o_ref, *, eps):
    x = x_ref[...]
    y = x * lax.rsqrt(jnp.mean(x * x, axis=-1, keepdims=True) + eps)
    o_ref[...] = (y * w_ref[...]).astype(o_ref.dtype)


def _rmsnorm(x, w, eps, out_dtype):
    m, d = x.shape
    tm = _pick_tile(m, 256, SUBLANES_BF16)
    return pl.pallas_call(
        functools.partial(_rmsnorm_kernel, eps=eps),
        grid=(m // tm,),
        in_specs=[pl.BlockSpec((tm, d), lambda i: (i, 0)),
                  pl.BlockSpec((1, d), lambda i: (0, 0))],
        out_specs=pl.BlockSpec((tm, d), lambda i: (i, 0)),
        out_shape=jax.ShapeDtypeStruct((m, d), out_dtype),
        compiler_params=_cparams("arbitrary"),
        name="rmsnorm",
    )(x, w.reshape(1, d).astype(F32))


def _inproj_kernel(h_ref, w_ref, cs_ref, o_ref, *, tiles_per_group):
    j = pl.program_id(1)
    acc = jnp.dot(h_ref[...], w_ref[...], preferred_element_type=F32)
    group = (j // tiles_per_group) % 3

    @pl.when(group == 2)
    def _():
        o_ref[...] = acc.astype(o_ref.dtype)

    @pl.when(group != 2)
    def _():
        cos = cs_ref[0, 0]
        sin = cs_ref[0, 1]
        for hh in range(acc.shape[1] // HEAD_DIM):
            cols = slice(hh * HEAD_DIM, (hh + 1) * HEAD_DIM)
            a = acc[:, cols]
            rot = pltpu.roll(a, HEAD_DIM // 2, 1)
            o_ref[:, cols] = (a * cos + rot * sin).astype(o_ref.dtype)


def _inproj(h, w, tables, seq, group_width):
    m, d = h.shape
    n = w.shape[1]
    tm = _pick_tile(seq, 1024, SUBLANES_BF16)
    tn = _pick_tile(group_width, 512, HEAD_DIM)
    tiles_per_group = group_width // tn
    seq_tiles = seq // tm

    def table_map(i, j):
        return (jnp.minimum((j // tiles_per_group) % 3, 1), 0, i % seq_tiles, 0)

    return pl.pallas_call(
        functools.partial(_inproj_kernel, tiles_per_group=tiles_per_group),
        grid=(m // tm, n // tn),
        in_specs=[pl.BlockSpec((tm, d), lambda i, j: (i, 0)),
                  pl.BlockSpec((d, tn), lambda i, j: (0, j)),
                  pl.BlockSpec((1, 2, tm, HEAD_DIM), table_map)],
        out_specs=pl.BlockSpec((tm, tn), lambda i, j: (i, j)),
        out_shape=jax.ShapeDtypeStruct((m, n), BF16),
        compiler_params=_cparams("arbitrary", "arbitrary"),
        name="inproj_rope",
    )(h, w, tables)


def _rope_tables(seq, q_scale):
    inv = 1.0 / (ROPE_THETA ** (jnp.arange(0, HEAD_DIM, 2, dtype=F32) / HEAD_DIM))
    ang = jnp.arange(seq, dtype=F32)[:, None] * inv[None, :]
    cos, sin = jnp.cos(ang), jnp.sin(ang)
    cos_full = jnp.concatenate([cos, cos], axis=-1)
    sin_signed = jnp.concatenate([-sin, sin], axis=-1)
    k_tab = jnp.stack([cos_full, sin_signed])
    return jnp.stack([k_tab * q_scale, k_tab])


def _own_block_scores(k_own, q):
    blk = k_own.shape[0]
    s_t = lax.dot_general(k_own, q, _NT_DIMS, preferred_element_type=F32)
    k_pos = lax.broadcasted_iota(jnp.int32, (blk, blk), 0)
    q_pos = lax.broadcasted_iota(jnp.int32, (blk, blk), 1)
    return jnp.where(k_pos <= q_pos, s_t, NEG_INF)


def _col_max(parts):
    m = jnp.max(parts[0], axis=0, keepdims=True)
    for part in parts[1:]:
        m = jnp.maximum(m, jnp.max(part, axis=0, keepdims=True))
    return m


def _col_sum(parts):
    l = jnp.sum(parts[0], axis=0, keepdims=True)
    for part in parts[1:]:
        l = l + jnp.sum(part, axis=0, keepdims=True)
    return l


def _transpose_blocks(v_ref, vt_sc, n_blocks, blk):
    for n in range(n_blocks):
        rows = slice(n * blk, (n + 1) * blk)
        vt_sc[:, rows] = v_ref[rows, :].astype(F32).T.astype(BF16)


def _moba_gate_bias(kmean_hi_lo, q, a, topk):
    blk = MOBA_BLOCK
    nbp = kmean_hi_lo.shape[0] // 2
    g2 = lax.dot_general(kmean_hi_lo, q, _NT_DIMS, preferred_element_type=F32)
    g = g2[0:nbp, :] + g2[nbp:2 * nbp, :]
    row = lax.broadcasted_iota(jnp.int32, g.shape, 0)
    col = lax.broadcasted_iota(jnp.int32, g.shape, 1)
    valid = row < jnp.where(col >= blk, a + 1, a)
    biases = []
    for m in range(a + 1):
        gm = g[m:m + 1, :]
        beats = valid & ((g > gm) | ((g == gm) & (row < m)))
        rank = jnp.sum(beats.astype(F32), axis=0, keepdims=True)
        biases.append(jnp.where(rank < topk, 0.0, NEG_INF))
    return biases


def _moba_head(tile, q, k_ref, cols, vt, kmean_hi_lo, topk):
    blk = MOBA_BLOCK
    a = 2 * tile
    n_keys = (a + 2) * blk
    win = _own_block_scores(k_ref[a * blk:n_keys, cols], q)
    win_a, win_b = win[0:blk, :], win[blk:2 * blk, :]
    if a + 1 > topk:
        biases = _moba_gate_bias(kmean_hi_lo, q, a, topk)
        second = lax.broadcasted_iota(jnp.int32, (1, 2 * blk), 1) >= blk
        bias_a = jnp.where(second, biases[a], 0.0)
    else:
        biases, bias_a = [0.0] * a, 0.0
    m = jnp.maximum(jnp.max(win_a, axis=0, keepdims=True) + bias_a,
                    jnp.max(win_b, axis=0, keepdims=True))
    past = []
    if a > 0:
        s_past = lax.dot_general(k_ref[0:a * blk, cols], q, _NT_DIMS, preferred_element_type=F32)
        past = [s_past[j * blk:(j + 1) * blk, :] for j in range(a)]
        for part, bias in zip(past, biases):
            m = jnp.maximum(m, jnp.max(part, axis=0, keepdims=True) + bias)
    probs = [jnp.exp2(part + (bias - m)) for part, bias in zip(past, biases)]
    probs += [jnp.exp2(win_a + (bias_a - m)), jnp.exp2(win_b - m)]
    l = _col_sum(probs)
    p = jnp.concatenate(probs, axis=0)
    acc = jnp.dot(vt[:, 0:n_keys], p.astype(BF16), preferred_element_type=F32)
    return acc / l


def _moba_query_tile(tile, q_ref, k_ref, o_ref, vt_sc, kmean_sc, *, topk):
    for hh in range(q_ref.shape[1] // HEAD_DIM):
        cols = slice(hh * HEAD_DIM, (hh + 1) * HEAD_DIM)
        out = _moba_head(tile, q_ref[:, cols], k_ref, cols, vt_sc.at[cols, :], kmean_sc[hh], topk)
        o_ref[:, cols] = out.T.astype(o_ref.dtype)


def _moba_kernel(q_ref, k_ref, v_ref, o_ref, vt_sc, kmean_sc, *, n_blocks, topk):
    i = pl.program_id(2)
    blk = MOBA_BLOCK
    nbp = kmean_sc.shape[1] // 2
    seq = k_ref.shape[0]

    @pl.when(i == 0)
    def _():
        _transpose_blocks(v_ref, vt_sc, n_blocks, blk)
        row = lax.broadcasted_iota(jnp.int32, (nbp, seq), 0)
        col = lax.broadcasted_iota(jnp.int32, (nbp, seq), 1)
        lo_edge = row * blk
        avg = jnp.where((col >= lo_edge) & (col < lo_edge + blk), 1.0 / blk, 0.0).astype(BF16)
        kmean = jnp.dot(avg, k_ref[...], preferred_element_type=F32)
        hi = kmean.astype(BF16)
        lo = (kmean - hi.astype(F32)).astype(BF16)
        for hh in range(kmean_sc.shape[0]):
            cols = slice(hh * HEAD_DIM, (hh + 1) * HEAD_DIM)
            kmean_sc[hh, 0:nbp, :] = hi[:, cols]
            kmean_sc[hh, nbp:2 * nbp, :] = lo[:, cols]

    for tile in range(n_blocks // 2):
        pl.when(i == tile)(functools.partial(_moba_query_tile, tile, q_ref, k_ref, o_ref, vt_sc,
                                             kmean_sc, topk=topk))


_MOBA_HEADS_PER_STEP = 2
_QUERY_TILE = 2 * MOBA_BLOCK


def _moba_attention(qkv, batch, seq, heads):
    blk = MOBA_BLOCK
    hps = _MOBA_HEADS_PER_STEP
    assert seq % _QUERY_TILE == 0 and heads % hps == 0
    n_blocks = seq // blk
    n_tiles = seq // _QUERY_TILE
    nbp = -(-n_blocks // SUBLANES_BF16) * SUBLANES_BF16
    topk = min(MOBA_TOPK, n_blocks)
    width = hps * HEAD_DIM
    groups = heads // hps
    return pl.pallas_call(
        functools.partial(_moba_kernel, n_blocks=n_blocks, topk=topk),
        grid=(batch, groups, n_tiles),
        in_specs=[pl.BlockSpec((_QUERY_TILE, width), lambda b, h, i: (b * n_tiles + i, h)),
                  pl.BlockSpec((seq, width), lambda b, h, i: (b, groups + h)),
                  pl.BlockSpec((seq, width), lambda b, h, i: (b, 2 * groups + h))],
        out_specs=pl.BlockSpec((_QUERY_TILE, width), lambda b, h, i: (b * n_tiles + i, h)),
        out_shape=jax.ShapeDtypeStruct((batch * seq, heads * HEAD_DIM), BF16),
        scratch_shapes=[pltpu.VMEM((width, seq), BF16),
                        pltpu.VMEM((hps, 2 * nbp, HEAD_DIM), BF16)],
        compiler_params=_cparams("arbitrary", "arbitrary", "arbitrary"),
        name="moba_attention",
    )(qkv, qkv, qkv)


def _diff_query_tile(n, q_ref, k_ref, lam_refs, subln_ref, o_ref, vt_sc, *, lam_init):
    blk = q_ref.shape[0]
    lq1_ref, lk1_ref, lq2_ref, lk2_ref = lam_refs
    lam = (jnp.exp(jnp.sum(lq1_ref[...] * lk1_ref[...], axis=-1, keepdims=True))
           - jnp.exp(jnp.sum(lq2_ref[...] * lk2_ref[...], axis=-1, keepdims=True))
           + lam_init)
    probs, sums = [], []
    for c in range(2):
        cols = slice(c * HEAD_DIM, (c + 1) * HEAD_DIM)
        q = q_ref[:, cols]
        parts = []
        if n > 0:
            parts.append(lax.dot_general(k_ref[0:n * blk, cols], q, _NT_DIMS,
                                         preferred_element_type=F32))
        parts.append(_own_block_scores(k_ref[n * blk:(n + 1) * blk, cols], q))
        m = _col_max(parts)
        probs.append([jnp.exp2(part - m) for part in parts])
        sums.append(_col_sum(probs[c]))
    ratio = -lam * sums[0] / sums[1]
    a_parts = [p0 + ratio * p1 for p0, p1 in zip(*probs)]
    a = a_parts[0] if n == 0 else jnp.concatenate(a_parts, axis=0)
    d = jnp.dot(vt_sc[:, 0:(n + 1) * blk], a.astype(BF16), preferred_element_type=F32)
    d = d / sums[0]
    y = d * lax.rsqrt(jnp.mean(d * d, axis=0, keepdims=True) + SUBLN_EPS)
    y = (y * subln_ref[...]) * (1.0 - lam_init)
    o_ref[...] = y.T.astype(o_ref.dtype)


def _diff_kernel(q_ref, k_ref, v_ref, lq1_ref, lk1_ref, lq2_ref, lk2_ref, subln_ref, o_ref,
                 vt_sc, *, n_tiles, lam_init):
    i = pl.program_id(2)

    @pl.when(i == 0)
    def _():
        _transpose_blocks(v_ref, vt_sc, v_ref.shape[0] // MOBA_BLOCK, MOBA_BLOCK)

    lam_refs = (lq1_ref, lk1_ref, lq2_ref, lk2_ref)
    for n in range(n_tiles):
        pl.when(i == n)(functools.partial(_diff_query_tile, n, q_ref, k_ref, lam_refs,
                                          subln_ref, o_ref, vt_sc, lam_init=lam_init))


def _diff_attention(qkv, lq1, lk1, lq2, lk2, subln_w, batch, seq, heads, col0):
    blk = _QUERY_TILE
    vdim = 2 * HEAD_DIM
    assert seq % blk == 0 and col0 % vdim == 0
    n_blocks = seq // blk
    c0 = col0 // vdim
    vec = pl.BlockSpec((1, HEAD_DIM), lambda b, h, i: (0, 0))
    as_row = lambda t: t.reshape(1, HEAD_DIM).astype(F32)
    return pl.pallas_call(
        functools.partial(_diff_kernel, n_tiles=n_blocks, lam_init=DIFF_LAMBDA_INIT),
        grid=(batch, heads, n_blocks),
        in_specs=[pl.BlockSpec((blk, vdim), lambda b, h, i: (b * n_blocks + i, c0 + h)),
                  pl.BlockSpec((seq, vdim), lambda b, h, i: (b, c0 + heads + h)),
                  pl.BlockSpec((seq, vdim), lambda b, h, i: (b, c0 + 2 * heads + h)),
                  vec, vec, vec, vec,
                  pl.BlockSpec((vdim, 1), lambda b, h, i: (0, 0))],
        out_specs=pl.BlockSpec((blk, vdim), lambda b, h, i: (b * n_blocks + i, h)),
        out_shape=jax.ShapeDtypeStruct((batch * seq, heads * vdim), BF16),
        scratch_shapes=[pltpu.VMEM((vdim, seq), BF16)],
        compiler_params=_cparams("arbitrary", "arbitrary", "arbitrary"),
        name="diff_attention",
    )(qkv, qkv, qkv, as_row(lq1), as_row(lk1), as_row(lq2), as_row(lk2),
      subln_w.reshape(vdim, 1).astype(F32))


def _outproj_kernel(a_ref, b_ref, wa_ref, wb_ref, x_ref, o_ref):
    acc = jnp.dot(a_ref[...], wa_ref[...], preferred_element_type=F32)
    acc += jnp.dot(b_ref[...], wb_ref[...], preferred_element_type=F32)
    o_ref[...] = x_ref[...] + acc


def _outproj(a, b, wa, wb, x):
    m, ka = a.shape
    kb = b.shape[1]
    n = wa.shape[1]
    tm = _pick_tile(m, 1024, SUBLANES_BF16)
    tn = _pick_tile(n, 512, HEAD_DIM)
    return pl.pallas_call(
        _outproj_kernel,
        grid=(m // tm, n // tn),
        in_specs=[pl.BlockSpec((tm, ka), lambda i, j: (i, 0)),
                  pl.BlockSpec((tm, kb), lambda i, j: (i, 0)),
                  pl.BlockSpec((ka, tn), lambda i, j: (0, j)),
                  pl.BlockSpec((kb, tn), lambda i, j: (0, j)),
                  pl.BlockSpec((tm, tn), lambda i, j: (i, j))],
        out_specs=pl.BlockSpec((tm, tn), lambda i, j: (i, j)),
        out_shape=jax.ShapeDtypeStruct((m, n), F32),
        compiler_params=_cparams("arbitrary", "arbitrary"),
        name="outproj_residual",
    )(a, b, wa, wb, x)


_CONV_CHUNK_ROWS = 128


def _up_kernel(h_ref, wg_ref, wu_ref, pg_ref, pu_ref, o_ref,
               g_sc, u_sc, cg_sc, cu_sc, *, seq_tiles, col_tiles, n_tiles):
    t = pl.program_id(0)
    tm = h_ref.shape[0]
    halo = SUBLANES_F32
    slot = t % 2
    prev = 1 - slot
    tile = jnp.minimum(t, n_tiles - 1)
    i = tile // col_tiles
    j = tile % col_tiles
    seq_start = i % seq_tiles == 0

    @pl.when(t == 0)
    def _():
        g_sc[1] = jnp.zeros(g_sc.shape[1:], F32)
        u_sc[1] = jnp.zeros(u_sc.shape[1:], F32)

    for sc, carry in ((g_sc, cg_sc), (u_sc, cu_sc)):
        @pl.when(seq_start)
        def _(sc=sc):
            sc[slot, 0:halo, :] = jnp.zeros((halo, sc.shape[2]), F32)

        @pl.when(jnp.logical_not(seq_start))
        def _(sc=sc, carry=carry):
            sc[slot, 0:halo, :] = carry[j]

    h = h_ref[...]
    for sc, w_ref, carry in ((g_sc, wg_ref, cg_sc), (u_sc, wu_ref, cu_sc)):
        sc[slot, halo:halo + tm, :] = jnp.dot(h, w_ref[...], preferred_element_type=F32)
        carry[j] = sc[slot, tm:tm + halo, :]

    def conv(sc, p_ref, r0):
        rows = _CONV_CHUNK_ROWS
        out = p_ref[CONV_WIDTH:CONV_WIDTH + 1, :]
        for tap in range(CONV_WIDTH):
            off = r0 + halo - (CONV_WIDTH - 1) + tap
            out = out + sc[prev, off:off + rows, :] * p_ref[tap:tap + 1, :]
        return out

    for r0 in range(0, tm, _CONV_CHUNK_ROWS):
        g = conv(g_sc, pg_ref, r0)
        u = conv(u_sc, pu_ref, r0)
        o_ref[r0:r0 + _CONV_CHUNK_ROWS, :] = (g * jax.nn.sigmoid(g) * u).astype(o_ref.dtype)


def _up_conv_gate(h, wg, wu, pg, pu, seq):
    m, d = h.shape
    n = wg.shape[1]
    tm = _pick_tile(seq, 1024, _CONV_CHUNK_ROWS)
    tn = _pick_tile(n, 512, HEAD_DIM)
    halo = SUBLANES_F32
    col_tiles = n // tn
    n_tiles = (m // tm) * col_tiles
    project = lambda t: jnp.minimum(t, n_tiles - 1)
    finish = lambda t: jnp.maximum(t - 1, 0)
    w_spec = pl.BlockSpec((d, tn), lambda t: (0, project(t) % col_tiles))
    p_spec = pl.BlockSpec((CONV_WIDTH + 1, tn), lambda t: (0, finish(t) % col_tiles))
    return pl.pallas_call(
        functools.partial(_up_kernel, seq_tiles=seq // tm, col_tiles=col_tiles, n_tiles=n_tiles),
        grid=(n_tiles + 1,),
        in_specs=[pl.BlockSpec((tm, d), lambda t: (project(t) // col_tiles, 0)),
                  w_spec, w_spec, p_spec, p_spec],
        out_specs=pl.BlockSpec((tm, tn),
                               lambda t: (finish(t) // col_tiles, finish(t) % col_tiles)),
        out_shape=jax.ShapeDtypeStruct((m, n), BF16),
        scratch_shapes=[pltpu.VMEM((2, tm + halo, tn), F32),
                        pltpu.VMEM((2, tm + halo, tn), F32),
                        pltpu.VMEM((col_tiles, halo, tn), F32),
                        pltpu.VMEM((col_tiles, halo, tn), F32)],
        compiler_params=_cparams("arbitrary"),
        name="up_conv_gate",
    )(h, wg, wu, pg, pu)


def _down_kernel(a_ref, w_ref, x_ref, o_ref):
    part = jnp.dot(a_ref[...], w_ref[...], preferred_element_type=F32)

    @pl.when(pl.program_id(2) == 0)
    def _():
        o_ref[...] = x_ref[...] + part

    @pl.when(pl.program_id(2) != 0)
    def _():
        o_ref[...] += part


def _down_residual(a, w, x):
    m, k = a.shape
    n = w.shape[1]
    tm = _pick_tile(m, 1024, SUBLANES_BF16)
    tn = _pick_tile(n, 1024, HEAD_DIM)
    tk = _pick_tile(k, 2816, HEAD_DIM)
    return pl.pallas_call(
        _down_kernel,
        grid=(m // tm, n // tn, k // tk),
        in_specs=[pl.BlockSpec((tm, tk), lambda i, j, kk: (i, kk)),
                  pl.BlockSpec((tk, tn), lambda i, j, kk: (kk, j)),
                  pl.BlockSpec((tm, tn), lambda i, j, kk: (i, j))],
        out_specs=pl.BlockSpec((tm, tn), lambda i, j, kk: (i, j)),
        out_shape=jax.ShapeDtypeStruct((m, n), F32),
        compiler_params=_cparams("arbitrary", "arbitrary", "arbitrary"),
        name="down_residual",
    )(a, w, x)


def _pad_cols(t, n):
    return jnp.pad(t, ((0, 0), (0, n - t.shape[1])))


def kernel(x, attn_norm_w, w_in, diff_lambda_q1, diff_lambda_k1, diff_lambda_q2, diff_lambda_k2, diff_subln_w, w_out, ffn_norm_w, w_up, conv_w, conv_b, w_down, final_norm_w):
    batch, seq, d_model = x.shape
    depth = w_in.shape[0]
    assert depth == 1, "lambda_init is folded for a single layer"
    moba_heads = d_model // (2 * HEAD_DIM)
    diff_heads = d_model // (4 * HEAD_DIM)
    moba_width = moba_heads * HEAD_DIM
    d_ff = w_down.shape[1]
    ff_quantum = 1024 if d_ff >= 1024 else HEAD_DIM
    d_ff_pad = -(-d_ff // ff_quantum) * ff_quantum

    xf = x.reshape(batch * seq, d_model)
    tables = _rope_tables(seq, HEAD_DIM ** -0.5 * math.log2(math.e))
    for l in range(depth):
        h = _rmsnorm(xf, attn_norm_w[l], RMS_EPS, BF16)
        qkv = _inproj(h, w_in[l].astype(BF16), tables, seq, moba_width)
        moba = _moba_attention(qkv, batch, seq, moba_heads)
        diff = _diff_attention(qkv, diff_lambda_q1[l], diff_lambda_k1[l], diff_lambda_q2[l],
                               diff_lambda_k2[l], diff_subln_w[l], batch, seq, diff_heads,
                               3 * moba_width)
        w_o = w_out[l].astype(BF16)
        xf = _outproj(moba, diff, w_o[:moba_width], w_o[moba_width:], xf)

        h = _rmsnorm(xf, ffn_norm_w[l], RMS_EPS, BF16)
        conv_params = jnp.concatenate([conv_w[l], conv_b[l][None, :]], axis=0).astype(F32)
        wg = _pad_cols(w_up[l][:, :d_ff].astype(BF16), d_ff_pad)
        wu = _pad_cols(w_up[l][:, d_ff:].astype(BF16), d_ff_pad)
        pg = _pad_cols(conv_params[:, :d_ff], d_ff_pad)
        pu = _pad_cols(conv_params[:, d_ff:], d_ff_pad)
        act = _up_conv_gate(h, wg, wu, pg, pu, seq)
        w_d = jnp.pad(w_down[l].astype(BF16), ((0, d_ff_pad - d_ff), (0, 0)))
        xf = _down_residual(act, w_d, xf)
    out = _rmsnorm(xf, final_norm_w, RMS_EPS, x.dtype)
    return out.reshape(batch, seq, d_model)
```

```python
import functools
import math

import jax
import jax.numpy as jnp
from jax import lax
from jax.experimental import pallas as pl
from jax.experimental.pallas import tpu as pltpu

HEAD_DIM = 128
MOBA_BLOCK = 256
MOBA_TOPK = 3
ROPE_THETA = 10000.0
CONV_WIDTH = 3
RMS_EPS = 1e-6
SUBLN_EPS = 1e-5
NEG_INF = -1e30
DIFF_LAMBDA_INIT = 0.8 - 0.6 * math.exp(-0.3 * 0)

V7X_VMEM_LIMIT_BYTES = 56 * 1024 * 1024
SUBLANES_F32 = 8
SUBLANES_BF16 = 16

F32 = jnp.float32
BF16 = jnp.bfloat16
_NT_DIMS = (((1,), (1,)), ((), ()))


def _cparams(*semantics, **extra):
    return pltpu.CompilerParams(dimension_semantics=semantics,
                                vmem_limit_bytes=V7X_VMEM_LIMIT_BYTES, **extra)


def _pick_tile(n, target, quantum):
    t = min(n, target) // quantum * quantum
    while t > quantum and n % t:
        t -= quantum
    assert t >= quantum and n % t == 0, (n, target, quantum)
    return t


def _rmsnorm_kernel(x_ref, w_ref, o_ref, *, eps):
    x = x_ref[...]
    y = x * lax.rsqrt(jnp.mean(x * x, axis=-1, keepdims=True) + eps)
    o_ref[...] = (y * w_ref[...]).astype(o_ref.dtype)


def _rmsnorm(x, w, eps, out_dtype):
    m, d = x.shape
    tm = _pick_tile(m, 256, SUBLANES_BF16)
    return pl.pallas_call(
        functools.partial(_rmsnorm_kernel, eps=eps),
        grid=(m // tm,),
        in_specs=[pl.BlockSpec((tm, d), lambda i: (i, 0)),
                  pl.BlockSpec((1, d), lambda i: (0, 0))],
        out_specs=pl.BlockSpec((tm, d), lambda i: (i, 0)),
        out_shape=jax.ShapeDtypeStruct((m, d), out_dtype),
        compiler_params=_cparams("arbitrary"),
        name="rmsnorm",
    )(x, w.reshape(1, d).astype(F32))


def _on_parity(t, step):
    pl.when(t % 2 == 0)(functools.partial(step, 0, 1))
    pl.when(t % 2 == 1)(functools.partial(step, 1, 0))


def _inproj_kernel(h_ref, w_ref, cs_ref, o_ref, acc0_sc, acc1_sc, *, tiles_per_group, col_tiles):
    t = pl.program_id(0)
    accs = (acc0_sc, acc1_sc)
    finish_group = ((jnp.maximum(t - 1, 0) % col_tiles) // tiles_per_group) % 3

    @pl.when(t == 0)
    def _():
        acc1_sc[...] = jnp.zeros(acc1_sc.shape, F32)

    def step(rotate, project, finish):
        accs[project][...] = jnp.dot(h_ref[...], w_ref[...], preferred_element_type=F32)
        acc = accs[finish]
        if not rotate:
            o_ref[...] = acc[...].astype(o_ref.dtype)
            return
        cos = cs_ref[0, 0]
        sin = cs_ref[0, 1]
        for hh in range(acc.shape[1] // HEAD_DIM):
            cols = slice(hh * HEAD_DIM, (hh + 1) * HEAD_DIM)
            a = acc[:, cols]
            rot = pltpu.roll(a, HEAD_DIM // 2, 1)
            o_ref[:, cols] = (a * cos + rot * sin).astype(o_ref.dtype)

    pl.when(finish_group != 2)(lambda: _on_parity(t, functools.partial(step, True)))
    pl.when(finish_group == 2)(lambda: _on_parity(t, functools.partial(step, False)))


def _inproj(h, w, tables, seq, group_width):
    m, d = h.shape
    n = w.shape[1]
    tm = _pick_tile(seq, 1024, SUBLANES_BF16)
    tn = _pick_tile(group_width, 512, HEAD_DIM)
    tiles_per_group = group_width // tn
    seq_tiles = seq // tm
    col_tiles = n // tn
    n_tiles = (m // tm) * col_tiles
    project = lambda t: jnp.minimum(t, n_tiles - 1)
    finish = lambda t: jnp.maximum(t - 1, 0)

    def table_map(t):
        i, j = finish(t) // col_tiles, finish(t) % col_tiles
        return (jnp.minimum((j // tiles_per_group) % 3, 1), 0, i % seq_tiles, 0)

    return pl.pallas_call(
        functools.partial(_inproj_kernel, tiles_per_group=tiles_per_group, col_tiles=col_tiles),
        grid=(n_tiles + 1,),
        in_specs=[pl.BlockSpec((tm, d), lambda t: (project(t) // col_tiles, 0)),
                  pl.BlockSpec((d, tn), lambda t: (0, project(t) % col_tiles)),
                  pl.BlockSpec((1, 2, tm, HEAD_DIM), table_map)],
        out_specs=pl.BlockSpec((tm, tn), lambda t: (finish(t) // col_tiles, finish(t) % col_tiles)),
        out_shape=jax.ShapeDtypeStruct((m, n), BF16),
        scratch_shapes=[pltpu.VMEM((tm, tn), F32), pltpu.VMEM((tm, tn), F32)],
        compiler_params=_cparams("arbitrary"),
        name="inproj_rope",
    )(h, w, tables)


def _rope_tables(seq, q_scale):
    inv = 1.0 / (ROPE_THETA ** (jnp.arange(0, HEAD_DIM, 2, dtype=F32) / HEAD_DIM))
    ang = jnp.arange(seq, dtype=F32)[:, None] * inv[None, :]
    cos, sin = jnp.cos(ang), jnp.sin(ang)
    cos_full = jnp.concatenate([cos, cos], axis=-1)
    sin_signed = jnp.concatenate([-sin, sin], axis=-1)
    k_tab = jnp.stack([cos_full, sin_signed])
    return jnp.stack([k_tab * q_scale, k_tab])


def _own_block_scores(k_own, q):
    blk = k_own.shape[0]
    s_t = lax.dot_general(k_own, q, _NT_DIMS, preferred_element_type=F32)
    k_pos = lax.broadcasted_iota(jnp.int32, (blk, blk), 0)
    q_pos = lax.broadcasted_iota(jnp.int32, (blk, blk), 1)
    return jnp.where(k_pos <= q_pos, s_t, NEG_INF)


def _col_max(parts):
    m = jnp.max(parts[0], axis=0, keepdims=True)
    for part in parts[1:]:
        m = jnp.maximum(m, jnp.max(part, axis=0, keepdims=True))
    return m


def _col_sum(parts):
    l = jnp.sum(parts[0], axis=0, keepdims=True)
    for part in parts[1:]:
        l = l + jnp.sum(part, axis=0, keepdims=True)
    return l


def _transpose_blocks(v_ref, vt_sc, n_blocks, blk):
    for n in range(n_blocks):
        rows = slice(n * blk, (n + 1) * blk)
        vt_sc[:, rows] = v_ref[rows, :].astype(F32).T.astype(BF16)


def _moba_gate_bias(kmean_hi_lo, q, a, topk):
    blk = MOBA_BLOCK
    nbp = kmean_hi_lo.shape[0] // 2
    g2 = lax.dot_general(kmean_hi_lo, q, _NT_DIMS, preferred_element_type=F32)
    g = g2[0:nbp, :] + g2[nbp:2 * nbp, :]
    row = lax.broadcasted_iota(jnp.int32, g.shape, 0)
    col = lax.broadcasted_iota(jnp.int32, g.shape, 1)
    valid = row < jnp.where(col >= blk, a + 1, a)
    biases = []
    for m in range(a + 1):
        gm = g[m:m + 1, :]
        beats = valid & ((g > gm) | ((g == gm) & (row < m)))
        rank = jnp.sum(beats.astype(F32), axis=0, keepdims=True)
        biases.append(jnp.where(rank < topk, 0.0, NEG_INF))
    return biases


def _moba_head(tile, q, k_ref, cols, vt, kmean_hi_lo, topk):
    blk = MOBA_BLOCK
    a = 2 * tile
    n_keys = (a + 2) * blk
    win = _own_block_scores(k_ref[a * blk:n_keys, cols], q)
    win_a, win_b = win[0:blk, :], win[blk:2 * blk, :]
    if a + 1 > topk:
        biases = _moba_gate_bias(kmean_hi_lo, q, a, topk)
        second = lax.broadcasted_iota(jnp.int32, (1, 2 * blk), 1) >= blk
        bias_a = jnp.where(second, biases[a], 0.0)
    else:
        biases, bias_a = [0.0] * a, 0.0
    m = jnp.maximum(jnp.max(win_a, axis=0, keepdims=True) + bias_a,
                    jnp.max(win_b, axis=0, keepdims=True))
    past = []
    if a > 0:
        s_past = lax.dot_general(k_ref[0:a * blk, cols], q, _NT_DIMS, preferred_element_type=F32)
        past = [s_past[j * blk:(j + 1) * blk, :] for j in range(a)]
        for part, bias in zip(past, biases):
            m = jnp.maximum(m, jnp.max(part, axis=0, keepdims=True) + bias)
    probs = [jnp.exp2(part + (bias - m)) for part, bias in zip(past, biases)]
    probs += [jnp.exp2(win_a + (bias_a - m)), jnp.exp2(win_b - m)]
    l = _col_sum(probs)
    p = jnp.concatenate(probs, axis=0)
    acc = jnp.dot(vt[:, 0:n_keys], p.astype(BF16), preferred_element_type=F32)
    return acc / l


def _moba_query_tile(tile, q_ref, k_ref, o_ref, vt_sc, kmean_sc, *, topk):
    for hh in range(q_ref.shape[1] // HEAD_DIM):
        cols = slice(hh * HEAD_DIM, (hh + 1) * HEAD_DIM)
        out = _moba_head(tile, q_ref[:, cols], k_ref, cols, vt_sc.at[cols, :], kmean_sc[hh], topk)
        o_ref[:, cols] = out.T.astype(o_ref.dtype)


def _moba_kernel(q_ref, k_ref, v_ref, o_ref, vt_sc, kmean_sc, *, n_blocks, topk):
    i = pl.program_id(2)
    blk = MOBA_BLOCK
    nbp = kmean_sc.shape[1] // 2
    seq = k_ref.shape[0]

    @pl.when(i == 0)
    def _():
        _transpose_blocks(v_ref, vt_sc, n_blocks, blk)
        row = lax.broadcasted_iota(jnp.int32, (nbp, seq), 0)
        col = lax.broadcasted_iota(jnp.int32, (nbp, seq), 1)
        lo_edge = row * blk
        avg = jnp.where((col >= lo_edge) & (col < lo_edge + blk), 1.0 / blk, 0.0).astype(BF16)
        kmean = jnp.dot(avg, k_ref[...], preferred_element_type=F32)
        hi = kmean.astype(BF16)
        lo = (kmean - hi.astype(F32)).astype(BF16)
        for hh in range(kmean_sc.shape[0]):
            cols = slice(hh * HEAD_DIM, (hh + 1) * HEAD_DIM)
            kmean_sc[hh, 0:nbp, :] = hi[:, cols]
            kmean_sc[hh, nbp:2 * nbp, :] = lo[:, cols]

    for tile in range(n_blocks // 2):
        pl.when(i == tile)(functools.partial(_moba_query_tile, tile, q_ref, k_ref, o_ref, vt_sc,
                                             kmean_sc, topk=topk))


_MOBA_HEADS_PER_STEP = 2
_QUERY_TILE = 2 * MOBA_BLOCK


def _moba_attention(qkv, batch, seq, heads):
    blk = MOBA_BLOCK
    hps = _MOBA_HEADS_PER_STEP
    assert seq % _QUERY_TILE == 0 and heads % hps == 0
    n_blocks = seq // blk
    n_tiles = seq // _QUERY_TILE
    nbp = -(-n_blocks // SUBLANES_BF16) * SUBLANES_BF16
    topk = min(MOBA_TOPK, n_blocks)
    width = hps * HEAD_DIM
    groups = heads // hps
    return pl.pallas_call(
        functools.partial(_moba_kernel, n_blocks=n_blocks, topk=topk),
        grid=(batch, groups, n_tiles),
        in_specs=[pl.BlockSpec((_QUERY_TILE, width), lambda b, h, i: (b * n_tiles + i, h)),
                  pl.BlockSpec((seq, width), lambda b, h, i: (b, groups + h)),
                  pl.BlockSpec((seq, width), lambda b, h, i: (b, 2 * groups + h))],
        out_specs=pl.BlockSpec((_QUERY_TILE, width), lambda b, h, i: (b * n_tiles + i, h)),
        out_shape=jax.ShapeDtypeStruct((batch * seq, heads * HEAD_DIM), BF16),
        scratch_shapes=[pltpu.VMEM((width, seq), BF16),
                        pltpu.VMEM((hps, 2 * nbp, HEAD_DIM), BF16)],
        compiler_params=_cparams("arbitrary", "arbitrary", "arbitrary"),
        name="moba_attention",
    )(qkv, qkv, qkv)


def _diff_query_tile(n, q_ref, k_ref, lam_refs, subln_ref, o_ref, vt_sc, *, lam_init):
    blk = q_ref.shape[0]
    lq1_ref, lk1_ref, lq2_ref, lk2_ref = lam_refs
    lam = (jnp.exp(jnp.sum(lq1_ref[...] * lk1_ref[...], axis=-1, keepdims=True))
           - jnp.exp(jnp.sum(lq2_ref[...] * lk2_ref[...], axis=-1, keepdims=True))
           + lam_init)
    probs, sums = [], []
    for c in range(2):
        cols = slice(c * HEAD_DIM, (c + 1) * HEAD_DIM)
        q = q_ref[:, cols]
        parts = []
        if n > 0:
            parts.append(lax.dot_general(k_ref[0:n * blk, cols], q, _NT_DIMS,
                                         preferred_element_type=F32))
        parts.append(_own_block_scores(k_ref[n * blk:(n + 1) * blk, cols], q))
        m = _col_max(parts)
        probs.append([jnp.exp2(part - m) for part in parts])
        sums.append(_col_sum(probs[c]))
    ratio = -lam * sums[0] / sums[1]
    a_parts = [p0 + ratio * p1 for p0, p1 in zip(*probs)]
    a = a_parts[0] if n == 0 else jnp.concatenate(a_parts, axis=0)
    d = jnp.dot(vt_sc[:, 0:(n + 1) * blk], a.astype(BF16), preferred_element_type=F32)
    d = d / sums[0]
    y = d * lax.rsqrt(jnp.mean(d * d, axis=0, keepdims=True) + SUBLN_EPS)
    y = (y * subln_ref[...]) * (1.0 - lam_init)
    o_ref[...] = y.T.astype(o_ref.dtype)


def _diff_kernel(q_ref, k_ref, v_ref, lq1_ref, lk1_ref, lq2_ref, lk2_ref, subln_ref, o_ref,
                 vt_sc, *, n_tiles, lam_init):
    i = pl.program_id(2)

    @pl.when(i == 0)
    def _():
        _transpose_blocks(v_ref, vt_sc, v_ref.shape[0] // MOBA_BLOCK, MOBA_BLOCK)

    lam_refs = (lq1_ref, lk1_ref, lq2_ref, lk2_ref)
    for n in range(n_tiles):
        pl.when(i == n)(functools.partial(_diff_query_tile, n, q_ref, k_ref, lam_refs,
                                          subln_ref, o_ref, vt_sc, lam_init=lam_init))


def _diff_attention(qkv, lq1, lk1, lq2, lk2, subln_w, batch, seq, heads, col0):
    blk = _QUERY_TILE
    vdim = 2 * HEAD_DIM
    assert seq % blk == 0 and col0 % vdim == 0
    n_blocks = seq // blk
    c0 = col0 // vdim
    vec = pl.BlockSpec((1, HEAD_DIM), lambda b, h, i: (0, 0))
    as_row = lambda t: t.reshape(1, HEAD_DIM).astype(F32)
    return pl.pallas_call(
        functools.partial(_diff_kernel, n_tiles=n_blocks, lam_init=DIFF_LAMBDA_INIT),
        grid=(batch, heads, n_blocks),
        in_specs=[pl.BlockSpec((blk, vdim), lambda b, h, i: (b * n_blocks + i, c0 + h)),
                  pl.BlockSpec((seq, vdim), lambda b, h, i: (b, c0 + heads + h)),
                  pl.BlockSpec((seq, vdim), lambda b, h, i: (b, c0 + 2 * heads + h)),
                  vec, vec, vec, vec,
                  pl.BlockSpec((vdim, 1), lambda b, h, i: (0, 0))],
        out_specs=pl.BlockSpec((blk, vdim), lambda b, h, i: (b * n_blocks + i, h)),
        out_shape=jax.ShapeDtypeStruct((batch * seq, heads * vdim), BF16),
        scratch_shapes=[pltpu.VMEM((vdim, seq), BF16)],
        compiler_params=_cparams("arbitrary", "arbitrary", "arbitrary"),
        name="diff_attention",
    )(qkv, qkv, qkv, as_row(lq1), as_row(lk1), as_row(lq2), as_row(lk2),
      subln_w.reshape(vdim, 1).astype(F32))


def _outproj_kernel(a_ref, b_ref, wa_ref, wb_ref, x_ref, o_ref):
    acc = jnp.dot(a_ref[...], wa_ref[...], preferred_element_type=F32)
    acc += jnp.dot(b_ref[...], wb_ref[...], preferred_element_type=F32)
    o_ref[...] = x_ref[...] + acc


def _outproj(a, b, wa, wb, x):
    m, ka = a.shape
    kb = b.shape[1]
    n = wa.shape[1]
    tm = _pick_tile(m, 1024, SUBLANES_BF16)
    tn = _pick_tile(n, 512, HEAD_DIM)
    return pl.pallas_call(
        _outproj_kernel,
        grid=(m // tm, n // tn),
        in_specs=[pl.BlockSpec((tm, ka), lambda i, j: (i, 0)),
                  pl.BlockSpec((tm, kb), lambda i, j: (i, 0)),
                  pl.BlockSpec((ka, tn), lambda i, j: (0, j)),
                  pl.BlockSpec((kb, tn), lambda i, j: (0, j)),
                  pl.BlockSpec((tm, tn), lambda i, j: (i, j))],
        out_specs=pl.BlockSpec((tm, tn), lambda i, j: (i, j)),
        out_shape=jax.ShapeDtypeStruct((m, n), F32),
        compiler_params=_cparams("arbitrary", "arbitrary"),
        name="outproj_residual",
    )(a, b, wa, wb, x)


_CONV_CHUNK_ROWS = 32
_UP_ROW_CHUNK = 256


def _up_kernel(h_ref, wg_ref, wu_ref, pg_ref, pu_ref, o_ref, g_sc, u_sc, cg_sc, cu_sc, *, seq_tiles):
    i = pl.program_id(0)
    j = pl.program_id(1)
    tm = h_ref.shape[0]
    halo = SUBLANES_F32
    seq_start = i % seq_tiles == 0
    pairs = ((g_sc, wg_ref, cg_sc), (u_sc, wu_ref, cu_sc))

    for sc, _, carry in pairs:
        @pl.when(seq_start)
        def _(sc=sc):
            sc[0:halo, :] = jnp.zeros((halo, sc.shape[1]), F32)

        @pl.when(jnp.logical_not(seq_start))
        def _(sc=sc, carry=carry):
            sc[0:halo, :] = carry[j]

    def conv(sc, p_ref, r0):
        out = p_ref[CONV_WIDTH:CONV_WIDTH + 1, :]
        for tap in range(CONV_WIDTH):
            off = r0 + halo - (CONV_WIDTH - 1) + tap
            out = out + sc[off:off + _CONV_CHUNK_ROWS, :] * p_ref[tap:tap + 1, :]
        return out

    for c0 in range(0, tm, _UP_ROW_CHUNK):
        h = h_ref[c0:c0 + _UP_ROW_CHUNK, :]
        for sc, w_ref, _ in pairs:
            sc[halo + c0:halo + c0 + _UP_ROW_CHUNK, :] = jnp.dot(
                h, w_ref[...], preferred_element_type=F32)
        for r0 in range(c0, c0 + _UP_ROW_CHUNK, _CONV_CHUNK_ROWS):
            g = conv(g_sc, pg_ref, r0)
            u = conv(u_sc, pu_ref, r0)
            o_ref[r0:r0 + _CONV_CHUNK_ROWS, :] = (g * jax.nn.sigmoid(g) * u).astype(o_ref.dtype)

    for sc, _, carry in pairs:
        carry[j] = sc[tm:tm + halo, :]


def _up_conv_gate(h, wg, wu, pg, pu, seq):
    m, d = h.shape
    n = wg.shape[1]
    tm = _pick_tile(seq, 1024, _UP_ROW_CHUNK)
    tn = _pick_tile(n, 512, HEAD_DIM)
    halo = SUBLANES_F32
    col = lambda i, j: (0, j)
    return pl.pallas_call(
        functools.partial(_up_kernel, seq_tiles=seq // tm),
        grid=(m // tm, n // tn),
        in_specs=[pl.BlockSpec((tm, d), lambda i, j: (i, 0)),
                  pl.BlockSpec((d, tn), col),
                  pl.BlockSpec((d, tn), col),
                  pl.BlockSpec((CONV_WIDTH + 1, tn), col),
                  pl.BlockSpec((CONV_WIDTH + 1, tn), col)],
        out_specs=pl.BlockSpec((tm, tn), lambda i, j: (i, j)),
        out_shape=jax.ShapeDtypeStruct((m, n), BF16),
        scratch_shapes=[pltpu.VMEM((tm + halo, tn), F32),
                        pltpu.VMEM((tm + halo, tn), F32),
                        pltpu.VMEM((n // tn, halo, tn), F32),
                        pltpu.VMEM((n // tn, halo, tn), F32)],
        compiler_params=_cparams("arbitrary", "arbitrary"),
        name="up_conv_gate",
    )(h, wg, wu, pg, pu)


def _down_kernel(a_ref, w_ref, x_ref, o_ref):
    part = jnp.dot(a_ref[...], w_ref[...], preferred_element_type=F32)

    @pl.when(pl.program_id(2) == 0)
    def _():
        o_ref[...] = x_ref[...] + part

    @pl.when(pl.program_id(2) != 0)
    def _():
        o_ref[...] += part


def _down_residual(a, w, x):
    m, k = a.shape
    n = w.shape[1]
    tm = _pick_tile(m, 1024, SUBLANES_BF16)
    tn = _pick_tile(n, 1024, HEAD_DIM)
    tk = _pick_tile(k, 2816, HEAD_DIM)
    return pl.pallas_call(
        _down_kernel,
        grid=(m // tm, n // tn, k // tk),
        in_specs=[pl.BlockSpec((tm, tk), lambda i, j, kk: (i, kk)),
                  pl.BlockSpec((tk, tn), lambda i, j, kk: (kk, j)),
                  pl.BlockSpec((tm, tn), lambda i, j, kk: (i, j))],
        out_specs=pl.BlockSpec((tm, tn), lambda i, j, kk: (i, j)),
        out_shape=jax.ShapeDtypeStruct((m, n), F32),
        compiler_params=_cparams("arbitrary", "arbitrary", "arbitrary"),
        name="down_residual",
    )(a, w, x)


_CAST_TILE = 256


def _cast_kernel(x_ref, o_ref, *, n_valid):
    t = pl.program_id(0)

    @pl.when(t < n_valid)
    def _():
        o_ref[...] = x_ref[...].astype(o_ref.dtype)

    @pl.when(t >= n_valid)
    def _():
        o_ref[...] = jnp.zeros(o_ref.shape, o_ref.dtype)


def _cast_pad(w, axis, start, size, padded):
    tile = _CAST_TILE if size % _CAST_TILE == 0 else HEAD_DIM
    assert start % tile == 0 and size % tile == 0 and padded % tile == 0
    first, n_valid = start // tile, size // tile
    other = w.shape[1 - axis]
    block = (tile, other) if axis == 0 else (other, tile)
    at = (lambda t: (t, 0)) if axis == 0 else (lambda t: (0, t))
    out_shape = (padded, other) if axis == 0 else (other, padded)
    return pl.pallas_call(
        functools.partial(_cast_kernel, n_valid=n_valid),
        grid=(padded // tile,),
        in_specs=[pl.BlockSpec(block, lambda t: at(first + jnp.minimum(t, n_valid - 1)))],
        out_specs=pl.BlockSpec(block, at),
        out_shape=jax.ShapeDtypeStruct(out_shape, BF16),
        compiler_params=_cparams("arbitrary"),
        name="cast_pad",
    )(w)


def _pad_cols(t, n):
    return jnp.pad(t, ((0, 0), (0, n - t.shape[1])))


def kernel(x, attn_norm_w, w_in, diff_lambda_q1, diff_lambda_k1, diff_lambda_q2, diff_lambda_k2, diff_subln_w, w_out, ffn_norm_w, w_up, conv_w, conv_b, w_down, final_norm_w):
    batch, seq, d_model = x.shape
    depth = w_in.shape[0]
    assert depth == 1, "lambda_init is folded for a single layer"
    moba_heads = d_model // (2 * HEAD_DIM)
    diff_heads = d_model // (4 * HEAD_DIM)
    moba_width = moba_heads * HEAD_DIM
    d_ff = w_down.shape[1]
    ff_quantum = 1024 if d_ff >= 1024 else HEAD_DIM
    d_ff_pad = -(-d_ff // ff_quantum) * ff_quantum

    xf = x.reshape(batch * seq, d_model)
    tables = _rope_tables(seq, HEAD_DIM ** -0.5 * math.log2(math.e))
    for l in range(depth):
        h = _rmsnorm(xf, attn_norm_w[l], RMS_EPS, BF16)
        qkv = _inproj(h, _cast_pad(w_in[l], 1, 0, w_in.shape[2], w_in.shape[2]), tables, seq,
                      moba_width)
        moba = _moba_attention(qkv, batch, seq, moba_heads)
        diff = _diff_attention(qkv, diff_lambda_q1[l], diff_lambda_k1[l], diff_lambda_q2[l],
                               diff_lambda_k2[l], diff_subln_w[l], batch, seq, diff_heads,
                               3 * moba_width)
        mix_width = w_out.shape[1]
        w_oa = _cast_pad(w_out[l], 0, 0, moba_width, moba_width)
        w_ob = _cast_pad(w_out[l], 0, moba_width, mix_width - moba_width, mix_width - moba_width)
        xf = _outproj(moba, diff, w_oa, w_ob, xf)

        h = _rmsnorm(xf, ffn_norm_w[l], RMS_EPS, BF16)
        conv_params = jnp.concatenate([conv_w[l], conv_b[l][None, :]], axis=0).astype(F32)
        wg = _cast_pad(w_up[l], 1, 0, d_ff, d_ff_pad)
        wu = _cast_pad(w_up[l], 1, d_ff, d_ff, d_ff_pad)
        pg = _pad_cols(conv_params[:, :d_ff], d_ff_pad)
        pu = _pad_cols(conv_params[:, d_ff:], d_ff_pad)
        act = _up_conv_gate(h, wg, wu, pg, pu, seq)
        xf = _down_residual(act, _cast_pad(w_down[l], 0, 0, d_ff, d_ff_pad), xf)
    out = _rmsnorm(xf, final_norm_w, RMS_EPS, x.dtype)
    return out.reshape(batch, seq, d_model)
```

```python
import functools
import math

import jax
import jax.numpy as jnp
from jax import lax
from jax.experimental import pallas as pl
from jax.experimental.pallas import tpu as pltpu

HEAD_DIM = 128
MOBA_BLOCK = 256
MOBA_TOPK = 3
ROPE_THETA = 10000.0
CONV_WIDTH = 3
RMS_EPS = 1e-6
SUBLN_EPS = 1e-5
NEG_INF = -1e30
DIFF_LAMBDA_INIT = 0.8 - 0.6 * math.exp(-0.3 * 0)

V7X_VMEM_LIMIT_BYTES = 56 * 1024 * 1024
SUBLANES_F32 = 8
SUBLANES_BF16 = 16

F32 = jnp.float32
BF16 = jnp.bfloat16
_NT_DIMS = (((1,), (1,)), ((), ()))


def _cparams(*semantics, **extra):
    return pltpu.CompilerParams(dimension_semantics=semantics,
                                vmem_limit_bytes=V7X_VMEM_LIMIT_BYTES, **extra)


def _pick_tile(n, target, quantum):
    t = min(n, target) // quantum * quantum
    while t > quantum and n % t:
        t -= quantum
    assert t >= quantum and n % t == 0, (n, target, quantum)
    return t


def _rmsnorm_kernel(x_ref, w_ref, o_ref, *, eps):
    x = x_ref[...]
    y = x * lax.rsqrt(jnp.mean(x * x, axis=-1, keepdims=True) + eps)
    o_ref[...] = (y * w_ref[...]).astype(o_ref.dtype)


def _rmsnorm(x, w, eps, out_dtype):
    m, d = x.shape
    tm = _pick_tile(m, 256, SUBLANES_BF16)
    return pl.pallas_call(
        functools.partial(_rmsnorm_kernel, eps=eps),
        grid=(m // tm,),
        in_specs=[pl.BlockSpec((tm, d), lambda i: (i, 0)),
                  pl.BlockSpec((1, d), lambda i: (0, 0))],
        out_specs=pl.BlockSpec((tm, d), lambda i: (i, 0)),
        out_shape=jax.ShapeDtypeStruct((m, d), out_dtype),
        compiler_params=_cparams("arbitrary"),
        name="rmsnorm",
    )(x, w.reshape(1, d).astype(F32))


def _on_parity(t, step):
    pl.when(t % 2 == 0)(functools.partial(step, 0, 1))
    pl.when(t % 2 == 1)(functools.partial(step, 1, 0))


def _inproj_kernel(h_ref, w_ref, cs_ref, o_ref, acc0_sc, acc1_sc, *, tiles_per_group, col_tiles):
    t = pl.program_id(0)
    accs = (acc0_sc, acc1_sc)
    finish_group = ((jnp.maximum(t - 1, 0) % col_tiles) // tiles_per_group) % 3

    @pl.when(t == 0)
    def _():
        acc1_sc[...] = jnp.zeros(acc1_sc.shape, F32)

    def step(rotate, project, finish):
        accs[project][...] = jnp.dot(h_ref[...], w_ref[...], preferred_element_type=F32)
        acc = accs[finish]
        if not rotate:
            o_ref[...] = acc[...].astype(o_ref.dtype)
            return
        cos = cs_ref[0, 0]
        sin = cs_ref[0, 1]
        for hh in range(acc.shape[1] // HEAD_DIM):
            cols = slice(hh * HEAD_DIM, (hh + 1) * HEAD_DIM)
            a = acc[:, cols]
            rot = pltpu.roll(a, HEAD_DIM // 2, 1)
            o_ref[:, cols] = (a * cos + rot * sin).astype(o_ref.dtype)

    pl.when(finish_group != 2)(lambda: _on_parity(t, functools.partial(step, True)))
    pl.when(finish_group == 2)(lambda: _on_parity(t, functools.partial(step, False)))


def _inproj(h, w, tables, seq, group_width):
    m, d = h.shape
    n = w.shape[1]
    tm = _pick_tile(seq, 1024, SUBLANES_BF16)
    tn = _pick_tile(group_width, 512, HEAD_DIM)
    tiles_per_group = group_width // tn
    seq_tiles = seq // tm
    col_tiles = n // tn
    n_tiles = (m // tm) * col_tiles
    project = lambda t: jnp.minimum(t, n_tiles - 1)
    finish = lambda t: jnp.maximum(t - 1, 0)

    def table_map(t):
        i, j = finish(t) // col_tiles, finish(t) % col_tiles
        return (jnp.minimum((j // tiles_per_group) % 3, 1), 0, i % seq_tiles, 0)

    return pl.pallas_call(
        functools.partial(_inproj_kernel, tiles_per_group=tiles_per_group, col_tiles=col_tiles),
        grid=(n_tiles + 1,),
        in_specs=[pl.BlockSpec((tm, d), lambda t: (project(t) // col_tiles, 0)),
                  pl.BlockSpec((d, tn), lambda t: (0, project(t) % col_tiles)),
                  pl.BlockSpec((1, 2, tm, HEAD_DIM), table_map)],
        out_specs=pl.BlockSpec((tm, tn), lambda t: (finish(t) // col_tiles, finish(t) % col_tiles)),
        out_shape=jax.ShapeDtypeStruct((m, n), BF16),
        scratch_shapes=[pltpu.VMEM((tm, tn), F32), pltpu.VMEM((tm, tn), F32)],
        compiler_params=_cparams("arbitrary"),
        name="inproj_rope",
    )(h, w, tables)


def _rope_tables(seq, q_scale):
    inv = 1.0 / (ROPE_THETA ** (jnp.arange(0, HEAD_DIM, 2, dtype=F32) / HEAD_DIM))
    ang = jnp.arange(seq, dtype=F32)[:, None] * inv[None, :]
    cos, sin = jnp.cos(ang), jnp.sin(ang)
    cos_full = jnp.concatenate([cos, cos], axis=-1)
    sin_signed = jnp.concatenate([-sin, sin], axis=-1)
    k_tab = jnp.stack([cos_full, sin_signed])
    return jnp.stack([k_tab * q_scale, k_tab])


def _own_block_scores(k_own, q):
    blk = k_own.shape[0]
    s_t = lax.dot_general(k_own, q, _NT_DIMS, preferred_element_type=F32)
    k_pos = lax.broadcasted_iota(jnp.int32, (blk, blk), 0)
    q_pos = lax.broadcasted_iota(jnp.int32, (blk, blk), 1)
    return jnp.where(k_pos <= q_pos, s_t, NEG_INF)


def _col_max(parts):
    m = jnp.max(parts[0], axis=0, keepdims=True)
    for part in parts[1:]:
        m = jnp.maximum(m, jnp.max(part, axis=0, keepdims=True))
    return m


def _col_sum(parts):
    l = jnp.sum(parts[0], axis=0, keepdims=True)
    for part in parts[1:]:
        l = l + jnp.sum(part, axis=0, keepdims=True)
    return l


def _transpose_blocks(v_ref, vt_sc, n_blocks, blk):
    for n in range(n_blocks):
        rows = slice(n * blk, (n + 1) * blk)
        vt_sc[:, rows] = v_ref[rows, :].astype(F32).T.astype(BF16)


def _moba_gate_bias(kmean_hi_lo, q, a, topk):
    blk = MOBA_BLOCK
    nbp = kmean_hi_lo.shape[0] // 2
    g2 = lax.dot_general(kmean_hi_lo, q, _NT_DIMS, preferred_element_type=F32)
    g = g2[0:nbp, :] + g2[nbp:2 * nbp, :]
    row = lax.broadcasted_iota(jnp.int32, g.shape, 0)
    col = lax.broadcasted_iota(jnp.int32, g.shape, 1)
    valid = row < jnp.where(col >= blk, a + 1, a)
    biases = []
    for m in range(a + 1):
        gm = g[m:m + 1, :]
        beats = valid & ((g > gm) | ((g == gm) & (row < m)))
        rank = jnp.sum(beats.astype(F32), axis=0, keepdims=True)
        biases.append(jnp.where(rank < topk, 0.0, NEG_INF))
    return biases


def _moba_head(tile, q, k_ref, cols, vt, kmean_hi_lo, topk):
    blk = MOBA_BLOCK
    a = 2 * tile
    n_keys = (a + 2) * blk
    win = _own_block_scores(k_ref[a * blk:n_keys, cols], q)
    win_a, win_b = win[0:blk, :], win[blk:2 * blk, :]
    if a + 1 > topk:
        biases = _moba_gate_bias(kmean_hi_lo, q, a, topk)
        second = lax.broadcasted_iota(jnp.int32, (1, 2 * blk), 1) >= blk
        bias_a = jnp.where(second, biases[a], 0.0)
    else:
        biases, bias_a = [0.0] * a, 0.0
    m = jnp.maximum(jnp.max(win_a, axis=0, keepdims=True) + bias_a,
                    jnp.max(win_b, axis=0, keepdims=True))
    past = []
    if a > 0:
        s_past = lax.dot_general(k_ref[0:a * blk, cols], q, _NT_DIMS, preferred_element_type=F32)
        past = [s_past[j * blk:(j + 1) * blk, :] for j in range(a)]
        for part, bias in zip(past, biases):
            m = jnp.maximum(m, jnp.max(part, axis=0, keepdims=True) + bias)
    probs = [jnp.exp2(part + (bias - m)) for part, bias in zip(past, biases)]
    probs += [jnp.exp2(win_a + (bias_a - m)), jnp.exp2(win_b - m)]
    l = _col_sum(probs)
    p = jnp.concatenate(probs, axis=0)
    acc = jnp.dot(vt[:, 0:n_keys], p.astype(BF16), preferred_element_type=F32)
    return acc / l


def _moba_query_tile(tile, q_ref, k_ref, o_ref, vt_sc, kmean_sc, *, topk):
    for hh in range(q_ref.shape[1] // HEAD_DIM):
        cols = slice(hh * HEAD_DIM, (hh + 1) * HEAD_DIM)
        out = _moba_head(tile, q_ref[:, cols], k_ref, cols, vt_sc.at[cols, :], kmean_sc[hh], topk)
        o_ref[:, cols] = out.T.astype(o_ref.dtype)


def _moba_kernel(q_ref, k_ref, v_ref, o_ref, vt_sc, kmean_sc, *, n_blocks, topk):
    i = pl.program_id(2)
    blk = MOBA_BLOCK
    nbp = kmean_sc.shape[1] // 2
    seq = k_ref.shape[0]

    @pl.when(i == 0)
    def _():
        _transpose_blocks(v_ref, vt_sc, n_blocks, blk)
        row = lax.broadcasted_iota(jnp.int32, (nbp, seq), 0)
        col = lax.broadcasted_iota(jnp.int32, (nbp, seq), 1)
        lo_edge = row * blk
        avg = jnp.where((col >= lo_edge) & (col < lo_edge + blk), 1.0 / blk, 0.0).astype(BF16)
        kmean = jnp.dot(avg, k_ref[...], preferred_element_type=F32)
        hi = kmean.astype(BF16)
        lo = (kmean - hi.astype(F32)).astype(BF16)
        for hh in range(kmean_sc.shape[0]):
            cols = slice(hh * HEAD_DIM, (hh + 1) * HEAD_DIM)
            kmean_sc[hh, 0:nbp, :] = hi[:, cols]
            kmean_sc[hh, nbp:2 * nbp, :] = lo[:, cols]

    for tile in range(n_blocks // 2):
        pl.when(i == tile)(functools.partial(_moba_query_tile, tile, q_ref, k_ref, o_ref, vt_sc,
                                             kmean_sc, topk=topk))


_MOBA_HEADS_PER_STEP = 2
_QUERY_TILE = 2 * MOBA_BLOCK


def _moba_attention(qkv, batch, seq, heads):
    blk = MOBA_BLOCK
    hps = _MOBA_HEADS_PER_STEP
    assert seq % _QUERY_TILE == 0 and heads % hps == 0
    n_blocks = seq // blk
    n_tiles = seq // _QUERY_TILE
    nbp = -(-n_blocks // SUBLANES_BF16) * SUBLANES_BF16
    topk = min(MOBA_TOPK, n_blocks)
    width = hps * HEAD_DIM
    groups = heads // hps
    return pl.pallas_call(
        functools.partial(_moba_kernel, n_blocks=n_blocks, topk=topk),
        grid=(batch, groups, n_tiles),
        in_specs=[pl.BlockSpec((_QUERY_TILE, width), lambda b, h, i: (b * n_tiles + i, h)),
                  pl.BlockSpec((seq, width), lambda b, h, i: (b, groups + h)),
                  pl.BlockSpec((seq, width), lambda b, h, i: (b, 2 * groups + h))],
        out_specs=pl.BlockSpec((_QUERY_TILE, width), lambda b, h, i: (b * n_tiles + i, h)),
        out_shape=jax.ShapeDtypeStruct((batch * seq, heads * HEAD_DIM), BF16),
        scratch_shapes=[pltpu.VMEM((width, seq), BF16),
                        pltpu.VMEM((hps, 2 * nbp, HEAD_DIM), BF16)],
        compiler_params=_cparams("arbitrary", "arbitrary", "arbitrary"),
        name="moba_attention",
    )(qkv, qkv, qkv)


def _diff_query_tile(n, q_ref, k_ref, lam_refs, subln_ref, o_ref, vt_sc, *, lam_init):
    blk = q_ref.shape[0]
    lq1_ref, lk1_ref, lq2_ref, lk2_ref = lam_refs
    lam = (jnp.exp(jnp.sum(lq1_ref[...] * lk1_ref[...], axis=-1, keepdims=True))
           - jnp.exp(jnp.sum(lq2_ref[...] * lk2_ref[...], axis=-1, keepdims=True))
           + lam_init)
    probs, sums = [], []
    for c in range(2):
        cols = slice(c * HEAD_DIM, (c + 1) * HEAD_DIM)
        q = q_ref[:, cols]
        parts = []
        if n > 0:
            parts.append(lax.dot_general(k_ref[0:n * blk, cols], q, _NT_DIMS,
                                         preferred_element_type=F32))
        parts.append(_own_block_scores(k_ref[n * blk:(n + 1) * blk, cols], q))
        m = _col_max(parts)
        probs.append([jnp.exp2(part - m) for part in parts])
        sums.append(_col_sum(probs[c]))
    ratio = -lam * sums[0] / sums[1]
    a_parts = [p0 + ratio * p1 for p0, p1 in zip(*probs)]
    a = a_parts[0] if n == 0 else jnp.concatenate(a_parts, axis=0)
    d = jnp.dot(vt_sc[:, 0:(n + 1) * blk], a.astype(BF16), preferred_element_type=F32)
    d = d / sums[0]
    y = d * lax.rsqrt(jnp.mean(d * d, axis=0, keepdims=True) + SUBLN_EPS)
    y = (y * subln_ref[...]) * (1.0 - lam_init)
    o_ref[...] = y.T.astype(o_ref.dtype)


def _diff_kernel(q_ref, k_ref, v_ref, lq1_ref, lk1_ref, lq2_ref, lk2_ref, subln_ref, o_ref,
                 vt_sc, *, n_tiles, lam_init):
    i = pl.program_id(2)

    @pl.when(i == 0)
    def _():
        _transpose_blocks(v_ref, vt_sc, v_ref.shape[0] // MOBA_BLOCK, MOBA_BLOCK)

    lam_refs = (lq1_ref, lk1_ref, lq2_ref, lk2_ref)
    for n in range(n_tiles):
        pl.when(i == n)(functools.partial(_diff_query_tile, n, q_ref, k_ref, lam_refs,
                                          subln_ref, o_ref, vt_sc, lam_init=lam_init))


def _diff_attention(qkv, lq1, lk1, lq2, lk2, subln_w, batch, seq, heads, col0):
    blk = _QUERY_TILE
    vdim = 2 * HEAD_DIM
    assert seq % blk == 0 and col0 % vdim == 0
    n_blocks = seq // blk
    c0 = col0 // vdim
    vec = pl.BlockSpec((1, HEAD_DIM), lambda b, h, i: (0, 0))
    as_row = lambda t: t.reshape(1, HEAD_DIM).astype(F32)
    return pl.pallas_call(
        functools.partial(_diff_kernel, n_tiles=n_blocks, lam_init=DIFF_LAMBDA_INIT),
        grid=(batch, heads, n_blocks),
        in_specs=[pl.BlockSpec((blk, vdim), lambda b, h, i: (b * n_blocks + i, c0 + h)),
                  pl.BlockSpec((seq, vdim), lambda b, h, i: (b, c0 + heads + h)),
                  pl.BlockSpec((seq, vdim), lambda b, h, i: (b, c0 + 2 * heads + h)),
                  vec, vec, vec, vec,
                  pl.BlockSpec((vdim, 1), lambda b, h, i: (0, 0))],
        out_specs=pl.BlockSpec((blk, vdim), lambda b, h, i: (b * n_blocks + i, h)),
        out_shape=jax.ShapeDtypeStruct((batch * seq, heads * vdim), BF16),
        scratch_shapes=[pltpu.VMEM((vdim, seq), BF16)],
        compiler_params=_cparams("arbitrary", "arbitrary", "arbitrary"),
        name="diff_attention",
    )(qkv, qkv, qkv, as_row(lq1), as_row(lk1), as_row(lq2), as_row(lk2),
      subln_w.reshape(vdim, 1).astype(F32))


def _outproj_kernel(a_ref, b_ref, wa_ref, wb_ref, x_ref, o_ref):
    acc = jnp.dot(a_ref[...], wa_ref[...], preferred_element_type=F32)
    acc += jnp.dot(b_ref[...], wb_ref[...], preferred_element_type=F32)
    o_ref[...] = x_ref[...] + acc


def _outproj(a, b, wa, wb, x):
    m, ka = a.shape
    kb = b.shape[1]
    n = wa.shape[1]
    tm = _pick_tile(m, 1024, SUBLANES_BF16)
    tn = _pick_tile(n, 512, HEAD_DIM)
    return pl.pallas_call(
        _outproj_kernel,
        grid=(m // tm, n // tn),
        in_specs=[pl.BlockSpec((tm, ka), lambda i, j: (i, 0)),
                  pl.BlockSpec((tm, kb), lambda i, j: (i, 0)),
                  pl.BlockSpec((ka, tn), lambda i, j: (0, j)),
                  pl.BlockSpec((kb, tn), lambda i, j: (0, j)),
                  pl.BlockSpec((tm, tn), lambda i, j: (i, j))],
        out_specs=pl.BlockSpec((tm, tn), lambda i, j: (i, j)),
        out_shape=jax.ShapeDtypeStruct((m, n), F32),
        compiler_params=_cparams("arbitrary", "arbitrary"),
        name="outproj_residual",
    )(a, b, wa, wb, x)


_CONV_CHUNK_ROWS = 128


def _up_kernel(h_ref, wg_ref, wu_ref, pg_ref, pu_ref, o_ref,
               g_sc, u_sc, cg_sc, cu_sc, *, seq_tiles, col_tiles, n_tiles):
    t = pl.program_id(0)
    tm = h_ref.shape[0]
    halo = SUBLANES_F32
    slot = t % 2
    prev = 1 - slot
    tile = jnp.minimum(t, n_tiles - 1)
    i = tile // col_tiles
    j = tile % col_tiles
    seq_start = i % seq_tiles == 0

    @pl.when(t == 0)
    def _():
        g_sc[1] = jnp.zeros(g_sc.shape[1:], F32)
        u_sc[1] = jnp.zeros(u_sc.shape[1:], F32)

    for sc, carry in ((g_sc, cg_sc), (u_sc, cu_sc)):
        @pl.when(seq_start)
        def _(sc=sc):
            sc[slot, 0:halo, :] = jnp.zeros((halo, sc.shape[2]), F32)

        @pl.when(jnp.logical_not(seq_start))
        def _(sc=sc, carry=carry):
            sc[slot, 0:halo, :] = carry[j]

    h = h_ref[...]
    for sc, w_ref, carry in ((g_sc, wg_ref, cg_sc), (u_sc, wu_ref, cu_sc)):
        sc[slot, halo:halo + tm, :] = jnp.dot(h, w_ref[...], preferred_element_type=F32)
        carry[j] = sc[slot, tm:tm + halo, :]

    def conv(sc, p_ref, r0):
        rows = _CONV_CHUNK_ROWS
        out = p_ref[CONV_WIDTH:CONV_WIDTH + 1, :]
        for tap in range(CONV_WIDTH):
            off = r0 + halo - (CONV_WIDTH - 1) + tap
            out = out + sc[prev, off:off + rows, :] * p_ref[tap:tap + 1, :]
        return out

    for r0 in range(0, tm, _CONV_CHUNK_ROWS):
        g = conv(g_sc, pg_ref, r0)
        u = conv(u_sc, pu_ref, r0)
        o_ref[r0:r0 + _CONV_CHUNK_ROWS, :] = (g * jax.nn.sigmoid(g) * u).astype(o_ref.dtype)


def _up_conv_gate(h, wg, wu, pg, pu, seq):
    m, d = h.shape
    n = wg.shape[1]
    tm = _pick_tile(seq, 1024, _CONV_CHUNK_ROWS)
    tn = _pick_tile(n, 512, HEAD_DIM)
    halo = SUBLANES_F32
    col_tiles = n // tn
    n_tiles = (m // tm) * col_tiles
    project = lambda t: jnp.minimum(t, n_tiles - 1)
    finish = lambda t: jnp.maximum(t - 1, 0)
    w_spec = pl.BlockSpec((d, tn), lambda t: (0, project(t) % col_tiles))
    p_spec = pl.BlockSpec((CONV_WIDTH + 1, tn), lambda t: (0, finish(t) % col_tiles))
    return pl.pallas_call(
        functools.partial(_up_kernel, seq_tiles=seq // tm, col_tiles=col_tiles, n_tiles=n_tiles),
        grid=(n_tiles + 1,),
        in_specs=[pl.BlockSpec((tm, d), lambda t: (project(t) // col_tiles, 0)),
                  w_spec, w_spec, p_spec, p_spec],
        out_specs=pl.BlockSpec((tm, tn),
                               lambda t: (finish(t) // col_tiles, finish(t) % col_tiles)),
        out_shape=jax.ShapeDtypeStruct((m, n), BF16),
        scratch_shapes=[pltpu.VMEM((2, tm + halo, tn), F32),
                        pltpu.VMEM((2, tm + halo, tn), F32),
                        pltpu.VMEM((col_tiles, halo, tn), F32),
                        pltpu.VMEM((col_tiles, halo, tn), F32)],
        compiler_params=_cparams("arbitrary"),
        name="up_conv_gate",
    )(h, wg, wu, pg, pu)


def _down_kernel(a_ref, w_ref, x_ref, o_ref):
    part = jnp.dot(a_ref[...], w_ref[...], preferred_element_type=F32)

    @pl.when(pl.program_id(2) == 0)
    def _():
        o_ref[...] = x_ref[...] + part

    @pl.when(pl.program_id(2) != 0)
    def _():
        o_ref[...] += part


def _down_residual(a, w, x):
    m, k = a.shape
    n = w.shape[1]
    tm = _pick_tile(m, 1024, SUBLANES_BF16)
    tn = _pick_tile(n, 1024, HEAD_DIM)
    tk = _pick_tile(k, 2816, HEAD_DIM)
    return pl.pallas_call(
        _down_kernel,
        grid=(m // tm, n // tn, k // tk),
        in_specs=[pl.BlockSpec((tm, tk), lambda i, j, kk: (i, kk)),
                  pl.BlockSpec((tk, tn), lambda i, j, kk: (kk, j)),
                  pl.BlockSpec((tm, tn), lambda i, j, kk: (i, j))],
        out_specs=pl.BlockSpec((tm, tn), lambda i, j, kk: (i, j)),
        out_shape=jax.ShapeDtypeStruct((m, n), F32),
        compiler_params=_cparams("arbitrary", "arbitrary", "arbitrary"),
        name="down_residual",
    )(a, w, x)


_CAST_TILE = 256


def _cast_kernel(x_ref, o_ref, *, n_valid):
    t = pl.program_id(0)

    @pl.when(t < n_valid)
    def _():
        o_ref[...] = x_ref[...].astype(o_ref.dtype)

    @pl.when(t >= n_valid)
    def _():
        o_ref[...] = jnp.zeros(o_ref.shape, o_ref.dtype)


def _cast_pad(w, axis, start, size, padded):
    tile = _CAST_TILE if size % _CAST_TILE == 0 else HEAD_DIM
    assert start % tile == 0 and size % tile == 0 and padded % tile == 0
    first, n_valid = start // tile, size // tile
    other = w.shape[1 - axis]
    block = (tile, other) if axis == 0 else (other, tile)
    at = (lambda t: (t, 0)) if axis == 0 else (lambda t: (0, t))
    out_shape = (padded, other) if axis == 0 else (other, padded)
    return pl.pallas_call(
        functools.partial(_cast_kernel, n_valid=n_valid),
        grid=(padded // tile,),
        in_specs=[pl.BlockSpec(block, lambda t: at(first + jnp.minimum(t, n_valid - 1)))],
        out_specs=pl.BlockSpec(block, at),
        out_shape=jax.ShapeDtypeStruct(out_shape, BF16),
        compiler_params=_cparams("arbitrary"),
        name="cast_pad",
    )(w)


def _pad_cols(t, n):
    return jnp.pad(t, ((0, 0), (0, n - t.shape[1])))


def kernel(x, attn_norm_w, w_in, diff_lambda_q1, diff_lambda_k1, diff_lambda_q2, diff_lambda_k2, diff_subln_w, w_out, ffn_norm_w, w_up, conv_w, conv_b, w_down, final_norm_w):
    batch, seq, d_model = x.shape
    depth = w_in.shape[0]
    assert depth == 1, "lambda_init is folded for a single layer"
    moba_heads = d_model // (2 * HEAD_DIM)
    diff_heads = d_model // (4 * HEAD_DIM)
    moba_width = moba_heads * HEAD_DIM
    d_ff = w_down.shape[1]
    ff_quantum = 1024 if d_ff >= 1024 else HEAD_DIM
    d_ff_pad = -(-d_ff // ff_quantum) * ff_quantum

    xf = x.reshape(batch * seq, d_model)
    tables = _rope_tables(seq, HEAD_DIM ** -0.5 * math.log2(math.e))
    for l in range(depth):
        h = _rmsnorm(xf, attn_norm_w[l], RMS_EPS, BF16)
        qkv = _inproj(h, _cast_pad(w_in[l], 1, 0, w_in.shape[2], w_in.shape[2]), tables, seq,
                      moba_width)
        moba = _moba_attention(qkv, batch, seq, moba_heads)
        diff = _diff_attention(qkv, diff_lambda_q1[l], diff_lambda_k1[l], diff_lambda_q2[l],
                               diff_lambda_k2[l], diff_subln_w[l], batch, seq, diff_heads,
                               3 * moba_width)
        mix_width = w_out.shape[1]
        w_oa = _cast_pad(w_out[l], 0, 0, moba_width, moba_width)
        w_ob = _cast_pad(w_out[l], 0, moba_width, mix_width - moba_width, mix_width - moba_width)
        xf = _outproj(moba, diff, w_oa, w_ob, xf)

        h = _rmsnorm(xf, ffn_norm_w[l], RMS_EPS, BF16)
        conv_params = jnp.concatenate([conv_w[l], conv_b[l][None, :]], axis=0).astype(F32)
        wg = _cast_pad(w_up[l], 1, 0, d_ff, d_ff_pad)
        wu = _cast_pad(w_up[l], 1, d_ff, d_ff, d_ff_pad)
        pg = _pad_cols(conv_params[:, :d_ff], d_ff_pad)
        pu = _pad_cols(conv_params[:, d_ff:], d_ff_pad)
        act = _up_conv_gate(h, wg, wu, pg, pu, seq)
        xf = _down_residual(act, _cast_pad(w_down[l], 0, 0, d_ff, d_ff_pad), xf)
    out = _rmsnorm(xf, final_norm_w, RMS_EPS, x.dtype)
    return out.reshape(batch, seq, d_model)
```

```python
import functools
import math

import jax
import jax.numpy as jnp
from jax import lax
from jax.experimental import pallas as pl
from jax.experimental.pallas import tpu as pltpu

HEAD_DIM = 128
MOBA_BLOCK = 256
MOBA_TOPK = 3
ROPE_THETA = 10000.0
CONV_WIDTH = 3
RMS_EPS = 1e-6
SUBLN_EPS = 1e-5
NEG_INF = -1e30
DIFF_LAMBDA_INIT = 0.8 - 0.6 * math.exp(-0.3 * 0)

V7X_VMEM_LIMIT_BYTES = 56 * 1024 * 1024
SUBLANES_F32 = 8
SUBLANES_BF16 = 16

F32 = jnp.float32
BF16 = jnp.bfloat16
_NT_DIMS = (((1,), (1,)), ((), ()))


def _cparams(*semantics, **extra):
    return pltpu.CompilerParams(dimension_semantics=semantics,
                                vmem_limit_bytes=V7X_VMEM_LIMIT_BYTES, **extra)


def _pick_tile(n, target, quantum):
    t = min(n, target) // quantum * quantum
    while t > quantum and n % t:
        t -= quantum
    assert t >= quantum and n % t == 0, (n, target, quantum)
    return t


def _rmsnorm_kernel(x_ref, w_ref, o_ref, *, eps):
    x = x_ref[...]
    y = x * lax.rsqrt(jnp.mean(x * x, axis=-1, keepdims=True) + eps)
    o_ref[...] = (y * w_ref[...]).astype(o_ref.dtype)


def _rmsnorm(x, w, eps, out_dtype):
    m, d = x.shape
    tm = _pick_tile(m, 256, SUBLANES_BF16)
    return pl.pallas_call(
        functools.partial(_rmsnorm_kernel, eps=eps),
        grid=(m // tm,),
        in_specs=[pl.BlockSpec((tm, d), lambda i: (i, 0)),
                  pl.BlockSpec((1, d), lambda i: (0, 0))],
        out_specs=pl.BlockSpec((tm, d), lambda i: (i, 0)),
        out_shape=jax.ShapeDtypeStruct((m, d), out_dtype),
        compiler_params=_cparams("arbitrary"),
        name="rmsnorm",
    )(x, w.reshape(1, d).astype(F32))


def _on_parity(t, step):
    pl.when(t % 2 == 0)(functools.partial(step, 0, 1))
    pl.when(t % 2 == 1)(functools.partial(step, 1, 0))


def _inproj_kernel(h_ref, w_ref, cs_ref, o_ref, acc0_sc, acc1_sc, *, tiles_per_group, col_tiles):
    t = pl.program_id(0)
    accs = (acc0_sc, acc1_sc)
    finish_group = ((jnp.maximum(t - 1, 0) % col_tiles) // tiles_per_group) % 3

    @pl.when(t == 0)
    def _():
        acc1_sc[...] = jnp.zeros(acc1_sc.shape, F32)

    def step(rotate, project, finish):
        accs[project][...] = jnp.dot(h_ref[...], w_ref[...], preferred_element_type=F32)
        acc = accs[finish]
        if not rotate:
            o_ref[...] = acc[...].astype(o_ref.dtype)
            return
        cos = cs_ref[0, 0]
        sin = cs_ref[0, 1]
        for hh in range(acc.shape[1] // HEAD_DIM):
            cols = slice(hh * HEAD_DIM, (hh + 1) * HEAD_DIM)
            a = acc[:, cols]
            rot = pltpu.roll(a, HEAD_DIM // 2, 1)
            o_ref[:, cols] = (a * cos + rot * sin).astype(o_ref.dtype)

    pl.when(finish_group != 2)(lambda: _on_parity(t, functools.partial(step, True)))
    pl.when(finish_group == 2)(lambda: _on_parity(t, functools.partial(step, False)))


def _inproj(h, w, tables, seq, group_width):
    m, d = h.shape
    n = w.shape[1]
    tm = _pick_tile(seq, 1024, SUBLANES_BF16)
    tn = _pick_tile(group_width, 1024, HEAD_DIM)
    tiles_per_group = group_width // tn
    seq_tiles = seq // tm
    col_tiles = n // tn
    n_tiles = (m // tm) * col_tiles
    project = lambda t: jnp.minimum(t, n_tiles - 1)
    finish = lambda t: jnp.maximum(t - 1, 0)

    def table_map(t):
        i, j = finish(t) // col_tiles, finish(t) % col_tiles
        return (jnp.minimum((j // tiles_per_group) % 3, 1), 0, i % seq_tiles, 0)

    return pl.pallas_call(
        functools.partial(_inproj_kernel, tiles_per_group=tiles_per_group, col_tiles=col_tiles),
        grid=(n_tiles + 1,),
        in_specs=[pl.BlockSpec((tm, d), lambda t: (project(t) // col_tiles, 0)),
                  pl.BlockSpec((d, tn), lambda t: (0, project(t) % col_tiles)),
                  pl.BlockSpec((1, 2, tm, HEAD_DIM), table_map)],
        out_specs=pl.BlockSpec((tm, tn), lambda t: (finish(t) // col_tiles, finish(t) % col_tiles)),
        out_shape=jax.ShapeDtypeStruct((m, n), BF16),
        scratch_shapes=[pltpu.VMEM((tm, tn), F32), pltpu.VMEM((tm, tn), F32)],
        compiler_params=_cparams("arbitrary"),
        name="inproj_rope",
    )(h, w, tables)


def _rope_tables(seq, q_scale):
    inv = 1.0 / (ROPE_THETA ** (jnp.arange(0, HEAD_DIM, 2, dtype=F32) / HEAD_DIM))
    ang = jnp.arange(seq, dtype=F32)[:, None] * inv[None, :]
    cos, sin = jnp.cos(ang), jnp.sin(ang)
    cos_full = jnp.concatenate([cos, cos], axis=-1)
    sin_signed = jnp.concatenate([-sin, sin], axis=-1)
    k_tab = jnp.stack([cos_full, sin_signed])
    return jnp.stack([k_tab * q_scale, k_tab])


def _own_block_scores(k_own, q):
    blk = k_own.shape[0]
    s_t = lax.dot_general(k_own, q, _NT_DIMS, preferred_element_type=F32)
    k_pos = lax.broadcasted_iota(jnp.int32, (blk, blk), 0)
    q_pos = lax.broadcasted_iota(jnp.int32, (blk, blk), 1)
    return jnp.where(k_pos <= q_pos, s_t, NEG_INF)


def _col_max(parts):
    m = jnp.max(parts[0], axis=0, keepdims=True)
    for part in parts[1:]:
        m = jnp.maximum(m, jnp.max(part, axis=0, keepdims=True))
    return m


def _col_sum(parts):
    l = jnp.sum(parts[0], axis=0, keepdims=True)
    for part in parts[1:]:
        l = l + jnp.sum(part, axis=0, keepdims=True)
    return l


def _transpose_blocks(v_ref, vt_sc, n_blocks, blk):
    for n in range(n_blocks):
        rows = slice(n * blk, (n + 1) * blk)
        vt_sc[:, rows] = v_ref[rows, :].astype(F32).T.astype(BF16)


def _moba_gate_bias(kmean_hi_lo, q, a, topk):
    blk = MOBA_BLOCK
    nbp = kmean_hi_lo.shape[0] // 2
    g2 = lax.dot_general(kmean_hi_lo, q, _NT_DIMS, preferred_element_type=F32)
    g = g2[0:nbp, :] + g2[nbp:2 * nbp, :]
    row = lax.broadcasted_iota(jnp.int32, g.shape, 0)
    col = lax.broadcasted_iota(jnp.int32, g.shape, 1)
    valid = row < jnp.where(col >= blk, a + 1, a)
    biases = []
    for m in range(a + 1):
        gm = g[m:m + 1, :]
        beats = valid & ((g > gm) | ((g == gm) & (row < m)))
        rank = jnp.sum(beats.astype(F32), axis=0, keepdims=True)
        biases.append(jnp.where(rank < topk, 0.0, NEG_INF))
    return biases


def _moba_head(tile, q, k_ref, cols, vt, kmean_hi_lo, topk):
    blk = MOBA_BLOCK
    a = 2 * tile
    n_keys = (a + 2) * blk
    win = _own_block_scores(k_ref[a * blk:n_keys, cols], q)
    win_a, win_b = win[0:blk, :], win[blk:2 * blk, :]
    if a + 1 > topk:
        biases = _moba_gate_bias(kmean_hi_lo, q, a, topk)
        second = lax.broadcasted_iota(jnp.int32, (1, 2 * blk), 1) >= blk
        bias_a = jnp.where(second, biases[a], 0.0)
    else:
        biases, bias_a = [0.0] * a, 0.0
    m = jnp.maximum(jnp.max(win_a, axis=0, keepdims=True) + bias_a,
                    jnp.max(win_b, axis=0, keepdims=True))
    past = []
    if a > 0:
        s_past = lax.dot_general(k_ref[0:a * blk, cols], q, _NT_DIMS, preferred_element_type=F32)
        past = [s_past[j * blk:(j + 1) * blk, :] for j in range(a)]
        for part, bias in zip(past, biases):
            m = jnp.maximum(m, jnp.max(part, axis=0, keepdims=True) + bias)
    probs = [jnp.exp2(part + (bias - m)) for part, bias in zip(past, biases)]
    probs += [jnp.exp2(win_a + (bias_a - m)), jnp.exp2(win_b - m)]
    l = _col_sum(probs)
    p = jnp.concatenate(probs, axis=0)
    acc = jnp.dot(vt[:, 0:n_keys], p.astype(BF16), preferred_element_type=F32)
    return acc / l


def _moba_query_tile(tile, q_ref, k_ref, o_ref, vt_sc, kmean_sc, *, topk):
    for hh in range(q_ref.shape[1] // HEAD_DIM):
        cols = slice(hh * HEAD_DIM, (hh + 1) * HEAD_DIM)
        out = _moba_head(tile, q_ref[:, cols], k_ref, cols, vt_sc.at[cols, :], kmean_sc[hh], topk)
        o_ref[:, cols] = out.T.astype(o_ref.dtype)


def _moba_kernel(q_ref, k_ref, v_ref, o_ref, vt_sc, kmean_sc, *, n_blocks, topk):
    i = pl.program_id(2)
    blk = MOBA_BLOCK
    nbp = kmean_sc.shape[1] // 2
    seq = k_ref.shape[0]

    @pl.when(i == 0)
    def _():
        _transpose_blocks(v_ref, vt_sc, n_blocks, blk)
        row = lax.broadcasted_iota(jnp.int32, (nbp, seq), 0)
        col = lax.broadcasted_iota(jnp.int32, (nbp, seq), 1)
        lo_edge = row * blk
        avg = jnp.where((col >= lo_edge) & (col < lo_edge + blk), 1.0 / blk, 0.0).astype(BF16)
        kmean = jnp.dot(avg, k_ref[...], preferred_element_type=F32)
        hi = kmean.astype(BF16)
        lo = (kmean - hi.astype(F32)).astype(BF16)
        for hh in range(kmean_sc.shape[0]):
            cols = slice(hh * HEAD_DIM, (hh + 1) * HEAD_DIM)
            kmean_sc[hh, 0:nbp, :] = hi[:, cols]
            kmean_sc[hh, nbp:2 * nbp, :] = lo[:, cols]

    for tile in range(n_blocks // 2):
        pl.when(i == tile)(functools.partial(_moba_query_tile, tile, q_ref, k_ref, o_ref, vt_sc,
                                             kmean_sc, topk=topk))


_MOBA_HEADS_PER_STEP = 2
_QUERY_TILE = 2 * MOBA_BLOCK


def _moba_attention(qkv, batch, seq, heads):
    blk = MOBA_BLOCK
    hps = _MOBA_HEADS_PER_STEP
    assert seq % _QUERY_TILE == 0 and heads % hps == 0
    n_blocks = seq // blk
    n_tiles = seq // _QUERY_TILE
    nbp = -(-n_blocks // SUBLANES_BF16) * SUBLANES_BF16
    topk = min(MOBA_TOPK, n_blocks)
    width = hps * HEAD_DIM
    groups = heads // hps
    return pl.pallas_call(
        functools.partial(_moba_kernel, n_blocks=n_blocks, topk=topk),
        grid=(batch, groups, n_tiles),
        in_specs=[pl.BlockSpec((_QUERY_TILE, width), lambda b, h, i: (b * n_tiles + i, h)),
                  pl.BlockSpec((seq, width), lambda b, h, i: (b, groups + h)),
                  pl.BlockSpec((seq, width), lambda b, h, i: (b, 2 * groups + h))],
        out_specs=pl.BlockSpec((_QUERY_TILE, width), lambda b, h, i: (b * n_tiles + i, h)),
        out_shape=jax.ShapeDtypeStruct((batch * seq, heads * HEAD_DIM), BF16),
        scratch_shapes=[pltpu.VMEM((width, seq), BF16),
                        pltpu.VMEM((hps, 2 * nbp, HEAD_DIM), BF16)],
        compiler_params=_cparams("arbitrary", "arbitrary", "arbitrary"),
        name="moba_attention",
    )(qkv, qkv, qkv)


def _diff_query_tile(n, q_ref, k_ref, lam_refs, subln_ref, o_ref, vt_sc, *, lam_init):
    blk = q_ref.shape[0]
    lq1_ref, lk1_ref, lq2_ref, lk2_ref = lam_refs
    lam = (jnp.exp(jnp.sum(lq1_ref[...] * lk1_ref[...], axis=-1, keepdims=True))
           - jnp.exp(jnp.sum(lq2_ref[...] * lk2_ref[...], axis=-1, keepdims=True))
           + lam_init)
    probs, sums = [], []
    for c in range(2):
        cols = slice(c * HEAD_DIM, (c + 1) * HEAD_DIM)
        q = q_ref[:, cols]
        parts = []
        if n > 0:
            parts.append(lax.dot_general(k_ref[0:n * blk, cols], q, _NT_DIMS,
                                         preferred_element_type=F32))
        parts.append(_own_block_scores(k_ref[n * blk:(n + 1) * blk, cols], q))
        m = _col_max(parts)
        probs.append([jnp.exp2(part - m) for part in parts])
        sums.append(_col_sum(probs[c]))
    ratio = -lam * sums[0] / sums[1]
    a_parts = [p0 + ratio * p1 for p0, p1 in zip(*probs)]
    a = a_parts[0] if n == 0 else jnp.concatenate(a_parts, axis=0)
    d = jnp.dot(vt_sc[:, 0:(n + 1) * blk], a.astype(BF16), preferred_element_type=F32)
    d = d / sums[0]
    y = d * lax.rsqrt(jnp.mean(d * d, axis=0, keepdims=True) + SUBLN_EPS)
    y = (y * subln_ref[...]) * (1.0 - lam_init)
    o_ref[...] = y.T.astype(o_ref.dtype)


def _diff_kernel(q_ref, k_ref, v_ref, lq1_ref, lk1_ref, lq2_ref, lk2_ref, subln_ref, o_ref,
                 vt_sc, *, n_tiles, lam_init):
    i = pl.program_id(2)

    @pl.when(i == 0)
    def _():
        _transpose_blocks(v_ref, vt_sc, v_ref.shape[0] // MOBA_BLOCK, MOBA_BLOCK)

    lam_refs = (lq1_ref, lk1_ref, lq2_ref, lk2_ref)
    for n in range(n_tiles):
        pl.when(i == n)(functools.partial(_diff_query_tile, n, q_ref, k_ref, lam_refs,
                                          subln_ref, o_ref, vt_sc, lam_init=lam_init))


def _diff_attention(qkv, lq1, lk1, lq2, lk2, subln_w, batch, seq, heads, col0):
    blk = _QUERY_TILE
    vdim = 2 * HEAD_DIM
    assert seq % blk == 0 and col0 % vdim == 0
    n_blocks = seq // blk
    c0 = col0 // vdim
    vec = pl.BlockSpec((1, HEAD_DIM), lambda b, h, i: (0, 0))
    as_row = lambda t: t.reshape(1, HEAD_DIM).astype(F32)
    return pl.pallas_call(
        functools.partial(_diff_kernel, n_tiles=n_blocks, lam_init=DIFF_LAMBDA_INIT),
        grid=(batch, heads, n_blocks),
        in_specs=[pl.BlockSpec((blk, vdim), lambda b, h, i: (b * n_blocks + i, c0 + h)),
                  pl.BlockSpec((seq, vdim), lambda b, h, i: (b, c0 + heads + h)),
                  pl.BlockSpec((seq, vdim), lambda b, h, i: (b, c0 + 2 * heads + h)),
                  vec, vec, vec, vec,
                  pl.BlockSpec((vdim, 1), lambda b, h, i: (0, 0))],
        out_specs=pl.BlockSpec((blk, vdim), lambda b, h, i: (b * n_blocks + i, h)),
        out_shape=jax.ShapeDtypeStruct((batch * seq, heads * vdim), BF16),
        scratch_shapes=[pltpu.VMEM((vdim, seq), BF16)],
        compiler_params=_cparams("arbitrary", "arbitrary", "arbitrary"),
        name="diff_attention",
    )(qkv, qkv, qkv, as_row(lq1), as_row(lk1), as_row(lq2), as_row(lk2),
      subln_w.reshape(vdim, 1).astype(F32))


def _outproj_kernel(a_ref, b_ref, wa_ref, wb_ref, x_ref, o_ref):
    acc = jnp.dot(a_ref[...], wa_ref[...], preferred_element_type=F32)
    acc += jnp.dot(b_ref[...], wb_ref[...], preferred_element_type=F32)
    o_ref[...] = x_ref[...] + acc


def _outproj(a, b, wa, wb, x):
    m, ka = a.shape
    kb = b.shape[1]
    n = wa.shape[1]
    tm = _pick_tile(m, 1024, SUBLANES_BF16)
    tn = _pick_tile(n, 1024, HEAD_DIM)
    return pl.pallas_call(
        _outproj_kernel,
        grid=(m // tm, n // tn),
        in_specs=[pl.BlockSpec((tm, ka), lambda i, j: (i, 0)),
                  pl.BlockSpec((tm, kb), lambda i, j: (i, 0)),
                  pl.BlockSpec((ka, tn), lambda i, j: (0, j)),
                  pl.BlockSpec((kb, tn), lambda i, j: (0, j)),
                  pl.BlockSpec((tm, tn), lambda i, j: (i, j))],
        out_specs=pl.BlockSpec((tm, tn), lambda i, j: (i, j)),
        out_shape=jax.ShapeDtypeStruct((m, n), F32),
        compiler_params=_cparams("arbitrary", "arbitrary"),
        name="outproj_residual",
    )(a, b, wa, wb, x)


_CONV_CHUNK_ROWS = 128


def _up_kernel(h_ref, wg_ref, wu_ref, pg_ref, pu_ref, o_ref,
               g_sc, u_sc, cg_sc, cu_sc, *, seq_tiles, col_tiles, n_tiles):
    t = pl.program_id(0)
    tm = h_ref.shape[0]
    halo = SUBLANES_F32
    slot = t % 2
    prev = 1 - slot
    tile = jnp.minimum(t, n_tiles - 1)
    i = tile // col_tiles
    j = tile % col_tiles
    seq_start = i % seq_tiles == 0

    @pl.when(t == 0)
    def _():
        g_sc[1] = jnp.zeros(g_sc.shape[1:], F32)
        u_sc[1] = jnp.zeros(u_sc.shape[1:], F32)

    for sc, carry in ((g_sc, cg_sc), (u_sc, cu_sc)):
        @pl.when(seq_start)
        def _(sc=sc):
            sc[slot, 0:halo, :] = jnp.zeros((halo, sc.shape[2]), F32)

        @pl.when(jnp.logical_not(seq_start))
        def _(sc=sc, carry=carry):
            sc[slot, 0:halo, :] = carry[j]

    h = h_ref[...]
    for sc, w_ref, carry in ((g_sc, wg_ref, cg_sc), (u_sc, wu_ref, cu_sc)):
        sc[slot, halo:halo + tm, :] = jnp.dot(h, w_ref[...], preferred_element_type=F32)
        carry[j] = sc[slot, tm:tm + halo, :]

    def conv(sc, p_ref, r0):
        rows = _CONV_CHUNK_ROWS
        out = p_ref[CONV_WIDTH:CONV_WIDTH + 1, :]
        for tap in range(CONV_WIDTH):
            off = r0 + halo - (CONV_WIDTH - 1) + tap
            out = out + sc[prev, off:off + rows, :] * p_ref[tap:tap + 1, :]
        return out

    for r0 in range(0, tm, _CONV_CHUNK_ROWS):
        g = conv(g_sc, pg_ref, r0)
        u = conv(u_sc, pu_ref, r0)
        o_ref[r0:r0 + _CONV_CHUNK_ROWS, :] = (g * jax.nn.sigmoid(g) * u).astype(o_ref.dtype)


def _up_conv_gate(h, wg, wu, pg, pu, seq):
    m, d = h.shape
    n = wg.shape[1]
    tm = _pick_tile(seq, 1024, _CONV_CHUNK_ROWS)
    tn = _pick_tile(n, 512, HEAD_DIM)
    halo = SUBLANES_F32
    col_tiles = n // tn
    n_tiles = (m // tm) * col_tiles
    project = lambda t: jnp.minimum(t, n_tiles - 1)
    finish = lambda t: jnp.maximum(t - 1, 0)
    w_spec = pl.BlockSpec((d, tn), lambda t: (0, project(t) % col_tiles))
    p_spec = pl.BlockSpec((CONV_WIDTH + 1, tn), lambda t: (0, finish(t) % col_tiles))
    return pl.pallas_call(
        functools.partial(_up_kernel, seq_tiles=seq // tm, col_tiles=col_tiles, n_tiles=n_tiles),
        grid=(n_tiles + 1,),
        in_specs=[pl.BlockSpec((tm, d), lambda t: (project(t) // col_tiles, 0)),
                  w_spec, w_spec, p_spec, p_spec],
        out_specs=pl.BlockSpec((tm, tn),
                               lambda t: (finish(t) // col_tiles, finish(t) % col_tiles)),
        out_shape=jax.ShapeDtypeStruct((m, n), BF16),
        scratch_shapes=[pltpu.VMEM((2, tm + halo, tn), F32),
                        pltpu.VMEM((2, tm + halo, tn), F32),
                        pltpu.VMEM((col_tiles, halo, tn), F32),
                        pltpu.VMEM((col_tiles, halo, tn), F32)],
        compiler_params=_cparams("arbitrary"),
        name="up_conv_gate",
    )(h, wg, wu, pg, pu)


def _down_kernel(a_ref, w_ref, x_ref, o_ref):
    @pl.when(pl.program_id(2) == 0)
    def _():
        o_ref[...] = x_ref[...]

    o_ref[...] += jnp.dot(a_ref[...], w_ref[...], preferred_element_type=F32)


def _down_residual(a, w, x):
    m, k = a.shape
    n = w.shape[1]
    tm = _pick_tile(m, 1024, SUBLANES_BF16)
    tn = _pick_tile(n, 1024, HEAD_DIM)
    tk = _pick_tile(k, 2816, HEAD_DIM)
    return pl.pallas_call(
        _down_kernel,
        grid=(m // tm, n // tn, k // tk),
        in_specs=[pl.BlockSpec((tm, tk), lambda i, j, kk: (i, kk)),
                  pl.BlockSpec((tk, tn), lambda i, j, kk: (kk, j)),
                  pl.BlockSpec((tm, tn), lambda i, j, kk: (i, j))],
        out_specs=pl.BlockSpec((tm, tn), lambda i, j, kk: (i, j)),
        out_shape=jax.ShapeDtypeStruct((m, n), F32),
        compiler_params=_cparams("arbitrary", "arbitrary", "arbitrary"),
        name="down_residual",
    )(a, w, x)


_CAST_TILE = 256


def _cast_kernel(x_ref, o_ref, *, n_valid):
    t = pl.program_id(0)

    @pl.when(t < n_valid)
    def _():
        o_ref[...] = x_ref[...].astype(o_ref.dtype)

    @pl.when(t >= n_valid)
    def _():
        o_ref[...] = jnp.zeros(o_ref.shape, o_ref.dtype)


def _cast_pad(w, axis, start, size, padded):
    tile = _CAST_TILE if size % _CAST_TILE == 0 else HEAD_DIM
    assert start % tile == 0 and size % tile == 0 and padded % tile == 0
    first, n_valid = start // tile, size // tile
    other = w.shape[1 - axis]
    block = (tile, other) if axis == 0 else (other, tile)
    at = (lambda t: (t, 0)) if axis == 0 else (lambda t: (0, t))
    out_shape = (padded, other) if axis == 0 else (other, padded)
    return pl.pallas_call(
        functools.partial(_cast_kernel, n_valid=n_valid),
        grid=(padded // tile,),
        in_specs=[pl.BlockSpec(block, lambda t: at(first + jnp.minimum(t, n_valid - 1)))],
        out_specs=pl.BlockSpec(block, at),
        out_shape=jax.ShapeDtypeStruct(out_shape, BF16),
        compiler_params=_cparams("arbitrary"),
        name="cast_pad",
    )(w)


def _pad_cols(t, n):
    return jnp.pad(t, ((0, 0), (0, n - t.shape[1])))


def kernel(x, attn_norm_w, w_in, diff_lambda_q1, diff_lambda_k1, diff_lambda_q2, diff_lambda_k2, diff_subln_w, w_out, ffn_norm_w, w_up, conv_w, conv_b, w_down, final_norm_w):
    batch, seq, d_model = x.shape
    depth = w_in.shape[0]
    assert depth == 1, "lambda_init is folded for a single layer"
    moba_heads = d_model // (2 * HEAD_DIM)
    diff_heads = d_model // (4 * HEAD_DIM)
    moba_width = moba_heads * HEAD_DIM
    d_ff = w_down.shape[1]
    ff_quantum = 1024 if d_ff >= 1024 else HEAD_DIM
    d_ff_pad = -(-d_ff // ff_quantum) * ff_quantum

    xf = x.reshape(batch * seq, d_model)
    tables = _rope_tables(seq, HEAD_DIM ** -0.5 * math.log2(math.e))
    for l in range(depth):
        h = _rmsnorm(xf, attn_norm_w[l], RMS_EPS, BF16)
        qkv = _inproj(h, _cast_pad(w_in[l], 1, 0, w_in.shape[2], w_in.shape[2]), tables, seq,
                      moba_width)
        moba = _moba_attention(qkv, batch, seq, moba_heads)
        diff = _diff_attention(qkv, diff_lambda_q1[l], diff_lambda_k1[l], diff_lambda_q2[l],
                               diff_lambda_k2[l], diff_subln_w[l], batch, seq, diff_heads,
                               3 * moba_width)
        mix_width = w_out.shape[1]
        w_oa = _cast_pad(w_out[l], 0, 0, moba_width, moba_width)
        w_ob = _cast_pad(w_out[l], 0, moba_width, mix_width - moba_width, mix_width - moba_width)
        xf = _outproj(moba, diff, w_oa, w_ob, xf)

        h = _rmsnorm(xf, ffn_norm_w[l], RMS_EPS, BF16)
        conv_params = jnp.concatenate([conv_w[l], conv_b[l][None, :]], axis=0).astype(F32)
        wg = _cast_pad(w_up[l], 1, 0, d_ff, d_ff_pad)
        wu = _cast_pad(w_up[l], 1, d_ff, d_ff, d_ff_pad)
        pg = _pad_cols(conv_params[:, :d_ff], d_ff_pad)
        pu = _pad_cols(conv_params[:, d_ff:], d_ff_pad)
        act = _up_conv_gate(h, wg, wu, pg, pu, seq)
        xf = _down_residual(act, _cast_pad(w_down[l], 0, 0, d_ff, d_ff_pad), xf)
    out = _rmsnorm(xf, final_norm_w, RMS_EPS, x.dtype)
    return out.reshape(batch, seq, d_model)
```

```python
import functools
import math

import jax
import jax.numpy as jnp
from jax import lax
from jax.experimental import pallas as pl
from jax.experimental.pallas import tpu as pltpu

HEAD_DIM = 128
MOBA_BLOCK = 256
MOBA_TOPK = 3
ROPE_THETA = 10000.0
CONV_WIDTH = 3
RMS_EPS = 1e-6
SUBLN_EPS = 1e-5
NEG_INF = -1e30
DIFF_LAMBDA_INIT = 0.8 - 0.6 * math.exp(-0.3 * 0)

V7X_VMEM_LIMIT_BYTES = 56 * 1024 * 1024
SUBLANES_F32 = 8
SUBLANES_BF16 = 16

F32 = jnp.float32
BF16 = jnp.bfloat16
_NT_DIMS = (((1,), (1,)), ((), ()))


def _cparams(*semantics, **extra):
    return pltpu.CompilerParams(dimension_semantics=semantics,
                                vmem_limit_bytes=V7X_VMEM_LIMIT_BYTES, **extra)


def _pick_tile(n, target, quantum):
    t = min(n, target) // quantum * quantum
    while t > quantum and n % t:
        t -= quantum
    assert t >= quantum and n % t == 0, (n, target, quantum)
    return t


def _rmsnorm_kernel(x_ref, w_ref, o_ref, *, eps):
    x = x_ref[...]
    y = x * lax.rsqrt(jnp.mean(x * x, axis=-1, keepdims=True) + eps)
    o_ref[...] = (y * w_ref[...]).astype(o_ref.dtype)


def _rmsnorm(x, w, eps, out_dtype):
    m, d = x.shape
    tm = _pick_tile(m, 256, SUBLANES_BF16)
    return pl.pallas_call(
        functools.partial(_rmsnorm_kernel, eps=eps),
        grid=(m // tm,),
        in_specs=[pl.BlockSpec((tm, d), lambda i: (i, 0)),
                  pl.BlockSpec((1, d), lambda i: (0, 0))],
        out_specs=pl.BlockSpec((tm, d), lambda i: (i, 0)),
        out_shape=jax.ShapeDtypeStruct((m, d), out_dtype),
        compiler_params=_cparams("arbitrary"),
        name="rmsnorm",
    )(x, w.reshape(1, d).astype(F32))


def _on_parity(t, step):
    pl.when(t % 2 == 0)(functools.partial(step, 0, 1))
    pl.when(t % 2 == 1)(functools.partial(step, 1, 0))


def _inproj_kernel(h_ref, w_ref, cs_ref, o_ref, acc0_sc, acc1_sc, *, tiles_per_group, col_tiles):
    t = pl.program_id(0)
    accs = (acc0_sc, acc1_sc)
    finish_group = ((jnp.maximum(t - 1, 0) % col_tiles) // tiles_per_group) % 3

    @pl.when(t == 0)
    def _():
        acc1_sc[...] = jnp.zeros(acc1_sc.shape, F32)

    def step(rotate, project, finish):
        accs[project][...] = jnp.dot(h_ref[...], w_ref[...], preferred_element_type=F32)
        acc = accs[finish]
        if not rotate:
            o_ref[...] = acc[...].astype(o_ref.dtype)
            return
        cos = cs_ref[0, 0]
        sin = cs_ref[0, 1]
        for hh in range(acc.shape[1] // HEAD_DIM):
            cols = slice(hh * HEAD_DIM, (hh + 1) * HEAD_DIM)
            a = acc[:, cols]
            rot = pltpu.roll(a, HEAD_DIM // 2, 1)
            o_ref[:, cols] = (a * cos + rot * sin).astype(o_ref.dtype)

    pl.when(finish_group != 2)(lambda: _on_parity(t, functools.partial(step, True)))
    pl.when(finish_group == 2)(lambda: _on_parity(t, functools.partial(step, False)))


def _inproj(h, w, tables, seq, group_width):
    m, d = h.shape
    n = w.shape[1]
    tm = _pick_tile(seq, 1024, SUBLANES_BF16)
    tn = _pick_tile(group_width, 1024, HEAD_DIM)
    tiles_per_group = group_width // tn
    seq_tiles = seq // tm
    col_tiles = n // tn
    n_tiles = (m // tm) * col_tiles
    project = lambda t: jnp.minimum(t, n_tiles - 1)
    finish = lambda t: jnp.maximum(t - 1, 0)

    def table_map(t):
        i, j = finish(t) // col_tiles, finish(t) % col_tiles
        return (jnp.minimum((j // tiles_per_group) % 3, 1), 0, i % seq_tiles, 0)

    return pl.pallas_call(
        functools.partial(_inproj_kernel, tiles_per_group=tiles_per_group, col_tiles=col_tiles),
        grid=(n_tiles + 1,),
        in_specs=[pl.BlockSpec((tm, d), lambda t: (project(t) // col_tiles, 0)),
                  pl.BlockSpec((d, tn), lambda t: (0, project(t) % col_tiles)),
                  pl.BlockSpec((1, 2, tm, HEAD_DIM), table_map)],
        out_specs=pl.BlockSpec((tm, tn), lambda t: (finish(t) // col_tiles, finish(t) % col_tiles)),
        out_shape=jax.ShapeDtypeStruct((m, n), BF16),
        scratch_shapes=[pltpu.VMEM((tm, tn), F32), pltpu.VMEM((tm, tn), F32)],
        compiler_params=_cparams("arbitrary"),
        name="inproj_rope",
    )(h, w, tables)


def _rope_tables(seq, q_scale):
    inv = 1.0 / (ROPE_THETA ** (jnp.arange(0, HEAD_DIM, 2, dtype=F32) / HEAD_DIM))
    ang = jnp.arange(seq, dtype=F32)[:, None] * inv[None, :]
    cos, sin = jnp.cos(ang), jnp.sin(ang)
    cos_full = jnp.concatenate([cos, cos], axis=-1)
    sin_signed = jnp.concatenate([-sin, sin], axis=-1)
    k_tab = jnp.stack([cos_full, sin_signed])
    return jnp.stack([k_tab * q_scale, k_tab])


def _own_block_scores(k_own, q):
    blk = k_own.shape[0]
    s_t = lax.dot_general(k_own, q, _NT_DIMS, preferred_element_type=F32)
    k_pos = lax.broadcasted_iota(jnp.int32, (blk, blk), 0)
    q_pos = lax.broadcasted_iota(jnp.int32, (blk, blk), 1)
    return jnp.where(k_pos <= q_pos, s_t, NEG_INF)


def _col_max(parts):
    m = jnp.max(parts[0], axis=0, keepdims=True)
    for part in parts[1:]:
        m = jnp.maximum(m, jnp.max(part, axis=0, keepdims=True))
    return m


def _col_sum(parts):
    l = jnp.sum(parts[0], axis=0, keepdims=True)
    for part in parts[1:]:
        l = l + jnp.sum(part, axis=0, keepdims=True)
    return l


def _transpose_blocks(v_ref, vt_sc, n_blocks, blk):
    for n in range(n_blocks):
        rows = slice(n * blk, (n + 1) * blk)
        vt_sc[:, rows] = v_ref[rows, :].astype(F32).T.astype(BF16)


def _moba_gate_bias(kmean_hi_lo, q, a, topk):
    blk = MOBA_BLOCK
    nbp = kmean_hi_lo.shape[0] // 2
    g2 = lax.dot_general(kmean_hi_lo, q, _NT_DIMS, preferred_element_type=F32)
    g = g2[0:nbp, :] + g2[nbp:2 * nbp, :]
    row = lax.broadcasted_iota(jnp.int32, g.shape, 0)
    col = lax.broadcasted_iota(jnp.int32, g.shape, 1)
    valid = row < jnp.where(col >= blk, a + 1, a)
    biases = []
    for m in range(a + 1):
        gm = g[m:m + 1, :]
        beats = valid & ((g > gm) | ((g == gm) & (row < m)))
        rank = jnp.sum(beats.astype(F32), axis=0, keepdims=True)
        biases.append(jnp.where(rank < topk, 0.0, NEG_INF))
    return biases


def _moba_head(tile, q, k_ref, cols, vt, kmean_hi_lo, topk, o_ref):
    blk = MOBA_BLOCK
    a = 2 * tile
    n_keys = (a + 2) * blk
    win = _own_block_scores(k_ref[a * blk:n_keys, cols], q)
    win_a, win_b = win[0:blk, :], win[blk:2 * blk, :]
    if a + 1 > topk:
        biases = _moba_gate_bias(kmean_hi_lo, q, a, topk)
        second = lax.broadcasted_iota(jnp.int32, (1, 2 * blk), 1) >= blk
        bias_a = jnp.where(second, biases[a], 0.0)
    else:
        biases, bias_a = [0.0] * a, 0.0
    past = []
    if a > 0:
        s_past = lax.dot_general(k_ref[0:a * blk, cols], q, _NT_DIMS, preferred_element_type=F32)
        past = [s_past[j * blk:(j + 1) * blk, :] for j in range(a)]
    yield
    m = jnp.maximum(jnp.max(win_a, axis=0, keepdims=True) + bias_a,
                    jnp.max(win_b, axis=0, keepdims=True))
    for part, bias in zip(past, biases):
        m = jnp.maximum(m, jnp.max(part, axis=0, keepdims=True) + bias)
    yield
    probs = [jnp.exp2(part + (bias - m)) for part, bias in zip(past, biases)]
    probs += [jnp.exp2(win_a + (bias_a - m)), jnp.exp2(win_b - m)]
    l = _col_sum(probs)
    p = jnp.concatenate(probs, axis=0).astype(BF16)
    yield
    acc = jnp.dot(vt[:, 0:n_keys], p, preferred_element_type=F32)
    yield
    o_ref[:, cols] = (acc / l).T.astype(o_ref.dtype)


_ALTERNATE_BELOW_TILE = 6


def _run_phases(chains, alternate):
    if not alternate:
        for chain in chains:
            for _ in chain:
                pass
        return
    while chains:
        chains = [chain for chain in chains if next(chain, True) is None]


def _moba_query_tile(tile, q_ref, k_ref, o_ref, vt_sc, kmean_sc, *, topk):
    heads = []
    for hh in range(q_ref.shape[1] // HEAD_DIM):
        cols = slice(hh * HEAD_DIM, (hh + 1) * HEAD_DIM)
        heads.append(_moba_head(tile, q_ref[:, cols], k_ref, cols, vt_sc.at[cols, :], kmean_sc[hh],
                                topk, o_ref))
    _run_phases(heads, alternate=tile < _ALTERNATE_BELOW_TILE)


def _moba_kernel(q_ref, k_ref, v_ref, o_ref, vt_sc, kmean_sc, *, n_blocks, topk):
    i = pl.program_id(2)
    blk = MOBA_BLOCK
    nbp = kmean_sc.shape[1] // 2
    seq = k_ref.shape[0]

    @pl.when(i == 0)
    def _():
        _transpose_blocks(v_ref, vt_sc, n_blocks, blk)
        row = lax.broadcasted_iota(jnp.int32, (nbp, seq), 0)
        col = lax.broadcasted_iota(jnp.int32, (nbp, seq), 1)
        lo_edge = row * blk
        avg = jnp.where((col >= lo_edge) & (col < lo_edge + blk), 1.0 / blk, 0.0).astype(BF16)
        kmean = jnp.dot(avg, k_ref[...], preferred_element_type=F32)
        hi = kmean.astype(BF16)
        lo = (kmean - hi.astype(F32)).astype(BF16)
        for hh in range(kmean_sc.shape[0]):
            cols = slice(hh * HEAD_DIM, (hh + 1) * HEAD_DIM)
            kmean_sc[hh, 0:nbp, :] = hi[:, cols]
            kmean_sc[hh, nbp:2 * nbp, :] = lo[:, cols]

    for tile in range(n_blocks // 2):
        pl.when(i == tile)(functools.partial(_moba_query_tile, tile, q_ref, k_ref, o_ref, vt_sc,
                                             kmean_sc, topk=topk))


_MOBA_HEADS_PER_STEP = 2
_QUERY_TILE = 2 * MOBA_BLOCK


def _moba_attention(qkv, batch, seq, heads):
    blk = MOBA_BLOCK
    hps = _MOBA_HEADS_PER_STEP
    assert seq % _QUERY_TILE == 0 and heads % hps == 0
    n_blocks = seq // blk
    n_tiles = seq // _QUERY_TILE
    nbp = -(-n_blocks // SUBLANES_BF16) * SUBLANES_BF16
    topk = min(MOBA_TOPK, n_blocks)
    width = hps * HEAD_DIM
    groups = heads // hps
    return pl.pallas_call(
        functools.partial(_moba_kernel, n_blocks=n_blocks, topk=topk),
        grid=(batch, groups, n_tiles),
        in_specs=[pl.BlockSpec((_QUERY_TILE, width), lambda b, h, i: (b * n_tiles + i, h)),
                  pl.BlockSpec((seq, width), lambda b, h, i: (b, groups + h)),
                  pl.BlockSpec((seq, width), lambda b, h, i: (b, 2 * groups + h))],
        out_specs=pl.BlockSpec((_QUERY_TILE, width), lambda b, h, i: (b * n_tiles + i, h)),
        out_shape=jax.ShapeDtypeStruct((batch * seq, heads * HEAD_DIM), BF16),
        scratch_shapes=[pltpu.VMEM((width, seq), BF16),
                        pltpu.VMEM((hps, 2 * nbp, HEAD_DIM), BF16)],
        compiler_params=_cparams("arbitrary", "arbitrary", "arbitrary"),
        name="moba_attention",
    )(qkv, qkv, qkv)


def _diff_query_tile(n, q_ref, k_ref, lam_refs, subln_ref, o_ref, vt_sc, *, lam_init):
    blk = q_ref.shape[0]
    lq1_ref, lk1_ref, lq2_ref, lk2_ref = lam_refs
    lam = (jnp.exp(jnp.sum(lq1_ref[...] * lk1_ref[...], axis=-1, keepdims=True))
           - jnp.exp(jnp.sum(lq2_ref[...] * lk2_ref[...], axis=-1, keepdims=True))
           + lam_init)
    probs, sums = [], []
    for c in range(2):
        cols = slice(c * HEAD_DIM, (c + 1) * HEAD_DIM)
        q = q_ref[:, cols]
        parts = []
        if n > 0:
            parts.append(lax.dot_general(k_ref[0:n * blk, cols], q, _NT_DIMS,
                                         preferred_element_type=F32))
        parts.append(_own_block_scores(k_ref[n * blk:(n + 1) * blk, cols], q))
        m = _col_max(parts)
        probs.append([jnp.exp2(part - m) for part in parts])
        sums.append(_col_sum(probs[c]))
    ratio = -lam * sums[0] / sums[1]
    a_parts = [p0 + ratio * p1 for p0, p1 in zip(*probs)]
    a = a_parts[0] if n == 0 else jnp.concatenate(a_parts, axis=0)
    d = jnp.dot(vt_sc[:, 0:(n + 1) * blk], a.astype(BF16), preferred_element_type=F32)
    d = d / sums[0]
    y = d * lax.rsqrt(jnp.mean(d * d, axis=0, keepdims=True) + SUBLN_EPS)
    y = (y * subln_ref[...]) * (1.0 - lam_init)
    o_ref[...] = y.T.astype(o_ref.dtype)


def _diff_kernel(q_ref, k_ref, v_ref, lq1_ref, lk1_ref, lq2_ref, lk2_ref, subln_ref, o_ref,
                 vt_sc, *, n_tiles, lam_init):
    i = pl.program_id(2)

    @pl.when(i == 0)
    def _():
        _transpose_blocks(v_ref, vt_sc, v_ref.shape[0] // MOBA_BLOCK, MOBA_BLOCK)

    lam_refs = (lq1_ref, lk1_ref, lq2_ref, lk2_ref)
    for n in range(n_tiles):
        pl.when(i == n)(functools.partial(_diff_query_tile, n, q_ref, k_ref, lam_refs,
                                          subln_ref, o_ref, vt_sc, lam_init=lam_init))


def _diff_attention(qkv, lq1, lk1, lq2, lk2, subln_w, batch, seq, heads, col0):
    blk = _QUERY_TILE
    vdim = 2 * HEAD_DIM
    assert seq % blk == 0 and col0 % vdim == 0
    n_blocks = seq // blk
    c0 = col0 // vdim
    vec = pl.BlockSpec((1, HEAD_DIM), lambda b, h, i: (0, 0))
    as_row = lambda t: t.reshape(1, HEAD_DIM).astype(F32)
    return pl.pallas_call(
        functools.partial(_diff_kernel, n_tiles=n_blocks, lam_init=DIFF_LAMBDA_INIT),
        grid=(batch, heads, n_blocks),
        in_specs=[pl.BlockSpec((blk, vdim), lambda b, h, i: (b * n_blocks + i, c0 + h)),
                  pl.BlockSpec((seq, vdim), lambda b, h, i: (b, c0 + heads + h)),
                  pl.BlockSpec((seq, vdim), lambda b, h, i: (b, c0 + 2 * heads + h)),
                  vec, vec, vec, vec,
                  pl.BlockSpec((vdim, 1), lambda b, h, i: (0, 0))],
        out_specs=pl.BlockSpec((blk, vdim), lambda b, h, i: (b * n_blocks + i, h)),
        out_shape=jax.ShapeDtypeStruct((batch * seq, heads * vdim), BF16),
        scratch_shapes=[pltpu.VMEM((vdim, seq), BF16)],
        compiler_params=_cparams("arbitrary", "arbitrary", "arbitrary"),
        name="diff_attention",
    )(qkv, qkv, qkv, as_row(lq1), as_row(lk1), as_row(lq2), as_row(lk2),
      subln_w.reshape(vdim, 1).astype(F32))


def _outproj_kernel(a_ref, b_ref, wa_ref, wb_ref, x_ref, o_ref):
    acc = jnp.dot(a_ref[...], wa_ref[...], preferred_element_type=F32)
    acc += jnp.dot(b_ref[...], wb_ref[...], preferred_element_type=F32)
    o_ref[...] = x_ref[...] + acc


def _outproj(a, b, wa, wb, x):
    m, ka = a.shape
    kb = b.shape[1]
    n = wa.shape[1]
    tm = _pick_tile(m, 1024, SUBLANES_BF16)
    tn = _pick_tile(n, 1024, HEAD_DIM)
    return pl.pallas_call(
        _outproj_kernel,
        grid=(m // tm, n // tn),
        in_specs=[pl.BlockSpec((tm, ka), lambda i, j: (i, 0)),
                  pl.BlockSpec((tm, kb), lambda i, j: (i, 0)),
                  pl.BlockSpec((ka, tn), lambda i, j: (0, j)),
                  pl.BlockSpec((kb, tn), lambda i, j: (0, j)),
                  pl.BlockSpec((tm, tn), lambda i, j: (i, j))],
        out_specs=pl.BlockSpec((tm, tn), lambda i, j: (i, j)),
        out_shape=jax.ShapeDtypeStruct((m, n), F32),
        compiler_params=_cparams("arbitrary", "arbitrary"),
        name="outproj_residual",
    )(a, b, wa, wb, x)


_CONV_CHUNK_ROWS = 256


def _up_kernel(h_ref, wg_ref, wu_ref, pg_ref, pu_ref, o_ref,
               g_sc, u_sc, cg_sc, cu_sc, *, seq_tiles, col_tiles, n_tiles):
    t = pl.program_id(0)
    tm = h_ref.shape[0]
    halo = SUBLANES_F32
    slot = t % 2
    prev = 1 - slot
    tile = jnp.minimum(t, n_tiles - 1)
    i = tile // col_tiles
    j = tile % col_tiles
    seq_start = i % seq_tiles == 0

    @pl.when(t == 0)
    def _():
        g_sc[1] = jnp.zeros(g_sc.shape[1:], F32)
        u_sc[1] = jnp.zeros(u_sc.shape[1:], F32)

    for sc, carry in ((g_sc, cg_sc), (u_sc, cu_sc)):
        @pl.when(seq_start)
        def _(sc=sc):
            sc[slot, 0:halo, :] = jnp.zeros((halo, sc.shape[2]), F32)

        @pl.when(jnp.logical_not(seq_start))
        def _(sc=sc, carry=carry):
            sc[slot, 0:halo, :] = carry[j]

    def conv(sc, p_ref, r0):
        rows = _CONV_CHUNK_ROWS
        out = p_ref[CONV_WIDTH:CONV_WIDTH + 1, :]
        for tap in range(CONV_WIDTH):
            off = r0 + halo - (CONV_WIDTH - 1) + tap
            out = out + sc[prev, off:off + rows, :] * p_ref[tap:tap + 1, :]
        return out

    h = h_ref[...]
    g_sc[slot, halo:halo + tm, :] = jnp.dot(h, wg_ref[...], preferred_element_type=F32)
    cg_sc[j] = g_sc[slot, tm:tm + halo, :]

    for r0 in range(0, tm, _CONV_CHUNK_ROWS):
        g = conv(g_sc, pg_ref, r0)
        u = conv(u_sc, pu_ref, r0)
        o_ref[r0:r0 + _CONV_CHUNK_ROWS, :] = (g * jax.nn.sigmoid(g) * u).astype(o_ref.dtype)

    u_sc[slot, halo:halo + tm, :] = jnp.dot(h, wu_ref[...], preferred_element_type=F32)
    cu_sc[j] = u_sc[slot, tm:tm + halo, :]


def _up_conv_gate(h, wg, wu, pg, pu, seq):
    m, d = h.shape
    n = wg.shape[1]
    tm = _pick_tile(seq, 1024, _CONV_CHUNK_ROWS)
    tn = _pick_tile(n, 512, HEAD_DIM)
    halo = SUBLANES_F32
    col_tiles = n // tn
    n_tiles = (m // tm) * col_tiles
    project = lambda t: jnp.minimum(t, n_tiles - 1)
    finish = lambda t: jnp.maximum(t - 1, 0)
    w_spec = pl.BlockSpec((d, tn), lambda t: (0, project(t) % col_tiles))
    p_spec = pl.BlockSpec((CONV_WIDTH + 1, tn), lambda t: (0, finish(t) % col_tiles))
    return pl.pallas_call(
        functools.partial(_up_kernel, seq_tiles=seq // tm, col_tiles=col_tiles, n_tiles=n_tiles),
        grid=(n_tiles + 1,),
        in_specs=[pl.BlockSpec((tm, d), lambda t: (project(t) // col_tiles, 0)),
                  w_spec, w_spec, p_spec, p_spec],
        out_specs=pl.BlockSpec((tm, tn),
                               lambda t: (finish(t) // col_tiles, finish(t) % col_tiles)),
        out_shape=jax.ShapeDtypeStruct((m, n), BF16),
        scratch_shapes=[pltpu.VMEM((2, tm + halo, tn), F32),
                        pltpu.VMEM((2, tm + halo, tn), F32),
                        pltpu.VMEM((col_tiles, halo, tn), F32),
                        pltpu.VMEM((col_tiles, halo, tn), F32)],
        compiler_params=_cparams("arbitrary"),
        name="up_conv_gate",
    )(h, wg, wu, pg, pu)


def _down_kernel(a_ref, w_ref, x_ref, o_ref):
    @pl.when(pl.program_id(2) == 0)
    def _():
        o_ref[...] = x_ref[...]

    o_ref[...] += jnp.dot(a_ref[...], w_ref[...], preferred_element_type=F32)


def _down_residual(a, w, x):
    m, k = a.shape
    n = w.shape[1]
    tm = _pick_tile(m, 1024, SUBLANES_BF16)
    tn = _pick_tile(n, 1024, HEAD_DIM)
    tk = _pick_tile(k, 2816, HEAD_DIM)
    return pl.pallas_call(
        _down_kernel,
        grid=(m // tm, n // tn, k // tk),
        in_specs=[pl.BlockSpec((tm, tk), lambda i, j, kk: (i, kk)),
                  pl.BlockSpec((tk, tn), lambda i, j, kk: (kk, j)),
                  pl.BlockSpec((tm, tn), lambda i, j, kk: (i, j))],
        out_specs=pl.BlockSpec((tm, tn), lambda i, j, kk: (i, j)),
        out_shape=jax.ShapeDtypeStruct((m, n), F32),
        compiler_params=_cparams("arbitrary", "arbitrary", "arbitrary"),
        name="down_residual",
    )(a, w, x)


_CAST_TILE = 256


def _cast_kernel(x_ref, o_ref, *, n_valid):
    t = pl.program_id(0)

    @pl.when(t < n_valid)
    def _():
        o_ref[...] = x_ref[...].astype(o_ref.dtype)

    @pl.when(t >= n_valid)
    def _():
        o_ref[...] = jnp.zeros(o_ref.shape, o_ref.dtype)


def _cast_pad(w, axis, start, size, padded):
    tile = _CAST_TILE if size % _CAST_TILE == 0 else HEAD_DIM
    assert start % tile == 0 and size % tile == 0 and padded % tile == 0
    first, n_valid = start // tile, size // tile
    other = w.shape[1 - axis]
    block = (tile, other) if axis == 0 else (other, tile)
    at = (lambda t: (t, 0)) if axis == 0 else (lambda t: (0, t))
    out_shape = (padded, other) if axis == 0 else (other, padded)
    return pl.pallas_call(
        functools.partial(_cast_kernel, n_valid=n_valid),
        grid=(padded // tile,),
        in_specs=[pl.BlockSpec(block, lambda t: at(first + jnp.minimum(t, n_valid - 1)))],
        out_specs=pl.BlockSpec(block, at),
        out_shape=jax.ShapeDtypeStruct(out_shape, BF16),
        compiler_params=_cparams("arbitrary"),
        name="cast_pad",
    )(w)


def _pad_cols(t, n):
    return jnp.pad(t, ((0, 0), (0, n - t.shape[1])))


def kernel(x, attn_norm_w, w_in, diff_lambda_q1, diff_lambda_k1, diff_lambda_q2, diff_lambda_k2, diff_subln_w, w_out, ffn_norm_w, w_up, conv_w, conv_b, w_down, final_norm_w):
    batch, seq, d_model = x.shape
    depth = w_in.shape[0]
    assert depth == 1, "lambda_init is folded for a single layer"
    moba_heads = d_model // (2 * HEAD_DIM)
    diff_heads = d_model // (4 * HEAD_DIM)
    moba_width = moba_heads * HEAD_DIM
    d_ff = w_down.shape[1]
    ff_quantum = 1024 if d_ff >= 1024 else HEAD_DIM
    d_ff_pad = -(-d_ff // ff_quantum) * ff_quantum

    xf = x.reshape(batch * seq, d_model)
    tables = _rope_tables(seq, HEAD_DIM ** -0.5 * math.log2(math.e))
    for l in range(depth):
        h = _rmsnorm(xf, attn_norm_w[l], RMS_EPS, BF16)
        qkv = _inproj(h, _cast_pad(w_in[l], 1, 0, w_in.shape[2], w_in.shape[2]), tables, seq,
                      moba_width)
        moba = _moba_attention(qkv, batch, seq, moba_heads)
        diff = _diff_attention(qkv, diff_lambda_q1[l], diff_lambda_k1[l], diff_lambda_q2[l],
                               diff_lambda_k2[l], diff_subln_w[l], batch, seq, diff_heads,
                               3 * moba_width)
        mix_width = w_out.shape[1]
        w_oa = _cast_pad(w_out[l], 0, 0, moba_width, moba_width)
        w_ob = _cast_pad(w_out[l], 0, moba_width, mix_width - moba_width, mix_width - moba_width)
        xf = _outproj(moba, diff, w_oa, w_ob, xf)

        h = _rmsnorm(xf, ffn_norm_w[l], RMS_EPS, BF16)
        conv_params = jnp.concatenate([conv_w[l], conv_b[l][None, :]], axis=0).astype(F32)
        wg = _cast_pad(w_up[l], 1, 0, d_ff, d_ff_pad)
        wu = _cast_pad(w_up[l], 1, d_ff, d_ff, d_ff_pad)
        pg = _pad_cols(conv_params[:, :d_ff], d_ff_pad)
        pu = _pad_cols(conv_params[:, d_ff:], d_ff_pad)
        act = _up_conv_gate(h, wg, wu, pg, pu, seq)
        xf = _down_residual(act, _cast_pad(w_down[l], 0, 0, d_ff, d_ff_pad), xf)
    out = _rmsnorm(xf, final_norm_w, RMS_EPS, x.dtype)
    return out.reshape(batch, seq, d_model)
```

```python
import functools
import math

import jax
import jax.numpy as jnp
from jax import lax
from jax.experimental import pallas as pl
from jax.experimental.pallas import tpu as pltpu

HEAD_DIM = 128
MOBA_BLOCK = 256
MOBA_TOPK = 3
ROPE_THETA = 10000.0
CONV_WIDTH = 3
RMS_EPS = 1e-6
SUBLN_EPS = 1e-5
NEG_INF = -1e30
DIFF_LAMBDA_INIT = 0.8 - 0.6 * math.exp(-0.3 * 0)

V7X_VMEM_LIMIT_BYTES = 56 * 1024 * 1024
SUBLANES_F32 = 8
SUBLANES_BF16 = 16

F32 = jnp.float32
BF16 = jnp.bfloat16
_NT_DIMS = (((1,), (1,)), ((), ()))


def _cparams(*semantics, **extra):
    return pltpu.CompilerParams(dimension_semantics=semantics,
                                vmem_limit_bytes=V7X_VMEM_LIMIT_BYTES, **extra)


def _pick_tile(n, target, quantum):
    t = min(n, target) // quantum * quantum
    while t > quantum and n % t:
        t -= quantum
    assert t >= quantum and n % t == 0, (n, target, quantum)
    return t


def _rmsnorm_kernel(x_ref, w_ref, o_ref, *, eps):
    x = x_ref[...]
    y = x * lax.rsqrt(jnp.mean(x * x, axis=-1, keepdims=True) + eps)
    o_ref[...] = (y * w_ref[...]).astype(o_ref.dtype)


def _rmsnorm(x, w, eps, out_dtype):
    m, d = x.shape
    tm = _pick_tile(m, 256, SUBLANES_BF16)
    return pl.pallas_call(
        functools.partial(_rmsnorm_kernel, eps=eps),
        grid=(m // tm,),
        in_specs=[pl.BlockSpec((tm, d), lambda i: (i, 0)),
                  pl.BlockSpec((1, d), lambda i: (0, 0))],
        out_specs=pl.BlockSpec((tm, d), lambda i: (i, 0)),
        out_shape=jax.ShapeDtypeStruct((m, d), out_dtype),
        compiler_params=_cparams("arbitrary"),
        name="rmsnorm",
    )(x, w.reshape(1, d).astype(F32))


def _on_parity(t, step):
    pl.when(t % 2 == 0)(functools.partial(step, 0, 1))
    pl.when(t % 2 == 1)(functools.partial(step, 1, 0))


def _inproj_kernel(h_ref, w_ref, cs_ref, o_ref, acc0_sc, acc1_sc, *, tiles_per_group, col_tiles):
    t = pl.program_id(0)
    accs = (acc0_sc, acc1_sc)
    finish_group = ((jnp.maximum(t - 1, 0) % col_tiles) // tiles_per_group) % 3

    @pl.when(t == 0)
    def _():
        acc1_sc[...] = jnp.zeros(acc1_sc.shape, F32)

    def step(rotate, project, finish):
        accs[project][...] = jnp.dot(h_ref[...], w_ref[...], preferred_element_type=F32)
        acc = accs[finish]
        if not rotate:
            o_ref[...] = acc[...].astype(o_ref.dtype)
            return
        cos = cs_ref[0, 0]
        sin = cs_ref[0, 1]
        for hh in range(acc.shape[1] // HEAD_DIM):
            cols = slice(hh * HEAD_DIM, (hh + 1) * HEAD_DIM)
            a = acc[:, cols]
            rot = pltpu.roll(a, HEAD_DIM // 2, 1)
            o_ref[:, cols] = (a * cos + rot * sin).astype(o_ref.dtype)

    pl.when(finish_group != 2)(lambda: _on_parity(t, functools.partial(step, True)))
    pl.when(finish_group == 2)(lambda: _on_parity(t, functools.partial(step, False)))


def _inproj(h, w, tables, seq, group_width):
    m, d = h.shape
    n = w.shape[1]
    tm = _pick_tile(seq, 1024, SUBLANES_BF16)
    tn = _pick_tile(group_width, 1024, HEAD_DIM)
    tiles_per_group = group_width // tn
    seq_tiles = seq // tm
    col_tiles = n // tn
    n_tiles = (m // tm) * col_tiles
    project = lambda t: jnp.minimum(t, n_tiles - 1)
    finish = lambda t: jnp.maximum(t - 1, 0)

    def table_map(t):
        i, j = finish(t) // col_tiles, finish(t) % col_tiles
        return (jnp.minimum((j // tiles_per_group) % 3, 1), 0, i % seq_tiles, 0)

    return pl.pallas_call(
        functools.partial(_inproj_kernel, tiles_per_group=tiles_per_group, col_tiles=col_tiles),
        grid=(n_tiles + 1,),
        in_specs=[pl.BlockSpec((tm, d), lambda t: (project(t) // col_tiles, 0)),
                  pl.BlockSpec((d, tn), lambda t: (0, project(t) % col_tiles)),
                  pl.BlockSpec((1, 2, tm, HEAD_DIM), table_map)],
        out_specs=pl.BlockSpec((tm, tn), lambda t: (finish(t) // col_tiles, finish(t) % col_tiles)),
        out_shape=jax.ShapeDtypeStruct((m, n), BF16),
        scratch_shapes=[pltpu.VMEM((tm, tn), F32), pltpu.VMEM((tm, tn), F32)],
        compiler_params=_cparams("arbitrary"),
        name="inproj_rope",
    )(h, w, tables)


def _rope_tables(seq, q_scale):
    inv = 1.0 / (ROPE_THETA ** (jnp.arange(0, HEAD_DIM, 2, dtype=F32) / HEAD_DIM))
    ang = jnp.arange(seq, dtype=F32)[:, None] * inv[None, :]
    cos, sin = jnp.cos(ang), jnp.sin(ang)
    cos_full = jnp.concatenate([cos, cos], axis=-1)
    sin_signed = jnp.concatenate([-sin, sin], axis=-1)
    k_tab = jnp.stack([cos_full, sin_signed])
    return jnp.stack([k_tab * q_scale, k_tab])


def _own_block_scores(k_own, q):
    blk = k_own.shape[0]
    s_t = lax.dot_general(k_own, q, _NT_DIMS, preferred_element_type=F32)
    k_pos = lax.broadcasted_iota(jnp.int32, (blk, blk), 0)
    q_pos = lax.broadcasted_iota(jnp.int32, (blk, blk), 1)
    return jnp.where(k_pos <= q_pos, s_t, NEG_INF)


def _col_max(parts):
    m = jnp.max(parts[0], axis=0, keepdims=True)
    for part in parts[1:]:
        m = jnp.maximum(m, jnp.max(part, axis=0, keepdims=True))
    return m


def _col_sum(parts):
    l = jnp.sum(parts[0], axis=0, keepdims=True)
    for part in parts[1:]:
        l = l + jnp.sum(part, axis=0, keepdims=True)
    return l


def _transpose_blocks(v_ref, vt_sc, n_blocks, blk):
    for n in range(n_blocks):
        rows = slice(n * blk, (n + 1) * blk)
        vt_sc[:, rows] = v_ref[rows, :].astype(F32).T.astype(BF16)


def _moba_gate_bias(kmean_hi_lo, q, a, topk):
    blk = MOBA_BLOCK
    nbp = kmean_hi_lo.shape[0] // 2
    g2 = lax.dot_general(kmean_hi_lo, q, _NT_DIMS, preferred_element_type=F32)
    g = g2[0:nbp, :] + g2[nbp:2 * nbp, :]
    row = lax.broadcasted_iota(jnp.int32, g.shape, 0)
    col = lax.broadcasted_iota(jnp.int32, g.shape, 1)
    valid = row < jnp.where(col >= blk, a + 1, a)
    biases = []
    for m in range(a + 1):
        gm = g[m:m + 1, :]
        beats = valid & ((g > gm) | ((g == gm) & (row < m)))
        rank = jnp.sum(beats.astype(F32), axis=0, keepdims=True)
        biases.append(jnp.where(rank < topk, 0.0, NEG_INF))
    return biases


def _moba_head(tile, q, k_ref, cols, vt, kmean_hi_lo, topk, o_ref):
    blk = MOBA_BLOCK
    a = 2 * tile
    n_keys = (a + 2) * blk
    win = _own_block_scores(k_ref[a * blk:n_keys, cols], q)
    win_a, win_b = win[0:blk, :], win[blk:2 * blk, :]
    if a + 1 > topk:
        biases = _moba_gate_bias(kmean_hi_lo, q, a, topk)
        second = lax.broadcasted_iota(jnp.int32, (1, 2 * blk), 1) >= blk
        bias_a = jnp.where(second, biases[a], 0.0)
    else:
        biases, bias_a = [0.0] * a, 0.0
    past = []
    if a > 0:
        s_past = lax.dot_general(k_ref[0:a * blk, cols], q, _NT_DIMS, preferred_element_type=F32)
        past = [s_past[j * blk:(j + 1) * blk, :] for j in range(a)]
    yield
    m = jnp.maximum(jnp.max(win_a, axis=0, keepdims=True) + bias_a,
                    jnp.max(win_b, axis=0, keepdims=True))
    for part, bias in zip(past, biases):
        m = jnp.maximum(m, jnp.max(part, axis=0, keepdims=True) + bias)
    yield
    probs = [jnp.exp2(part + (bias - m)) for part, bias in zip(past, biases)]
    probs += [jnp.exp2(win_a + (bias_a - m)), jnp.exp2(win_b - m)]
    l = _col_sum(probs)
    p = jnp.concatenate(probs, axis=0).astype(BF16)
    yield
    acc = jnp.dot(vt[:, 0:n_keys], p, preferred_element_type=F32)
    yield
    o_ref[:, cols] = (acc / l).T.astype(o_ref.dtype)


_ALTERNATE_BELOW_TILE = 6


def _run_phases(chains, alternate):
    if not alternate:
        for chain in chains:
            for _ in chain:
                pass
        return
    while chains:
        chains = [chain for chain in chains if next(chain, True) is None]


def _moba_query_tile(tile, q_ref, k_ref, o_ref, vt_sc, kmean_sc, *, topk):
    heads = []
    for hh in range(q_ref.shape[1] // HEAD_DIM):
        cols = slice(hh * HEAD_DIM, (hh + 1) * HEAD_DIM)
        heads.append(_moba_head(tile, q_ref[:, cols], k_ref, cols, vt_sc.at[cols, :], kmean_sc[hh],
                                topk, o_ref))
    _run_phases(heads, alternate=tile < _ALTERNATE_BELOW_TILE)


def _moba_kernel(q_ref, k_ref, v_ref, o_ref, vt_sc, kmean_sc, *, n_blocks, topk):
    i = pl.program_id(2)
    blk = MOBA_BLOCK
    nbp = kmean_sc.shape[1] // 2
    seq = k_ref.shape[0]

    @pl.when(i == 0)
    def _():
        _transpose_blocks(v_ref, vt_sc, n_blocks, blk)
        row = lax.broadcasted_iota(jnp.int32, (nbp, seq), 0)
        col = lax.broadcasted_iota(jnp.int32, (nbp, seq), 1)
        lo_edge = row * blk
        avg = jnp.where((col >= lo_edge) & (col < lo_edge + blk), 1.0 / blk, 0.0).astype(BF16)
        kmean = jnp.dot(avg, k_ref[...], preferred_element_type=F32)
        hi = kmean.astype(BF16)
        lo = (kmean - hi.astype(F32)).astype(BF16)
        for hh in range(kmean_sc.shape[0]):
            cols = slice(hh * HEAD_DIM, (hh + 1) * HEAD_DIM)
            kmean_sc[hh, 0:nbp, :] = hi[:, cols]
            kmean_sc[hh, nbp:2 * nbp, :] = lo[:, cols]

    for tile in range(n_blocks // 2):
        pl.when(i == tile)(functools.partial(_moba_query_tile, tile, q_ref, k_ref, o_ref, vt_sc,
                                             kmean_sc, topk=topk))


_MOBA_HEADS_PER_STEP = 2
_QUERY_TILE = 2 * MOBA_BLOCK


def _moba_attention(qkv, batch, seq, heads):
    blk = MOBA_BLOCK
    hps = _MOBA_HEADS_PER_STEP
    assert seq % _QUERY_TILE == 0 and heads % hps == 0
    n_blocks = seq // blk
    n_tiles = seq // _QUERY_TILE
    nbp = -(-n_blocks // SUBLANES_BF16) * SUBLANES_BF16
    topk = min(MOBA_TOPK, n_blocks)
    width = hps * HEAD_DIM
    groups = heads // hps
    return pl.pallas_call(
        functools.partial(_moba_kernel, n_blocks=n_blocks, topk=topk),
        grid=(batch, groups, n_tiles),
        in_specs=[pl.BlockSpec((_QUERY_TILE, width), lambda b, h, i: (b * n_tiles + i, h)),
                  pl.BlockSpec((seq, width), lambda b, h, i: (b, groups + h)),
                  pl.BlockSpec((seq, width), lambda b, h, i: (b, 2 * groups + h))],
        out_specs=pl.BlockSpec((_QUERY_TILE, width), lambda b, h, i: (b * n_tiles + i, h)),
        out_shape=jax.ShapeDtypeStruct((batch * seq, heads * HEAD_DIM), BF16),
        scratch_shapes=[pltpu.VMEM((width, seq), BF16),
                        pltpu.VMEM((hps, 2 * nbp, HEAD_DIM), BF16)],
        compiler_params=_cparams("arbitrary", "arbitrary", "arbitrary"),
        name="moba_attention",
    )(qkv, qkv, qkv)


def _diff_head(n, q_ref, k_ref, hcols, lam, subln_ref, o_ref, vt, *, lam_init):
    blk = q_ref.shape[0]
    scores = []
    for c in range(2):
        cols = slice(hcols.start + c * HEAD_DIM, hcols.start + (c + 1) * HEAD_DIM)
        q = q_ref[:, cols]
        parts = []
        if n > 0:
            parts.append(lax.dot_general(k_ref[0:n * blk, cols], q, _NT_DIMS,
                                         preferred_element_type=F32))
        parts.append(_own_block_scores(k_ref[n * blk:(n + 1) * blk, cols], q))
        scores.append(parts)
    yield
    maxima = [_col_max(parts) for parts in scores]
    yield
    probs = [[jnp.exp2(part - m) for part in parts] for parts, m in zip(scores, maxima)]
    sums = [_col_sum(p) for p in probs]
    ratio = -lam * sums[0] / sums[1]
    a_parts = [p0 + ratio * p1 for p0, p1 in zip(*probs)]
    a = (a_parts[0] if n == 0 else jnp.concatenate(a_parts, axis=0)).astype(BF16)
    yield
    d = jnp.dot(vt[:, 0:(n + 1) * blk], a, preferred_element_type=F32)
    yield
    d = d / sums[0]
    y = d * lax.rsqrt(jnp.mean(d * d, axis=0, keepdims=True) + SUBLN_EPS)
    y = (y * subln_ref[...]) * (1.0 - lam_init)
    o_ref[:, hcols] = y.T.astype(o_ref.dtype)


def _diff_query_tile(n, q_ref, k_ref, lam_refs, subln_ref, o_ref, vt_sc, *, lam_init):
    vdim = 2 * HEAD_DIM
    lq1_ref, lk1_ref, lq2_ref, lk2_ref = lam_refs
    lam = (jnp.exp(jnp.sum(lq1_ref[...] * lk1_ref[...], axis=-1, keepdims=True))
           - jnp.exp(jnp.sum(lq2_ref[...] * lk2_ref[...], axis=-1, keepdims=True))
           + lam_init)
    heads = []
    for hh in range(q_ref.shape[1] // vdim):
        hcols = slice(hh * vdim, (hh + 1) * vdim)
        heads.append(_diff_head(n, q_ref, k_ref, hcols, lam, subln_ref, o_ref, vt_sc.at[hcols, :],
                                lam_init=lam_init))
    _run_phases(heads, alternate=n < _ALTERNATE_BELOW_TILE)


def _diff_kernel(q_ref, k_ref, v_ref, lq1_ref, lk1_ref, lq2_ref, lk2_ref, subln_ref, o_ref,
                 vt_sc, *, n_tiles, lam_init):
    i = pl.program_id(2)

    @pl.when(i == 0)
    def _():
        _transpose_blocks(v_ref, vt_sc, v_ref.shape[0] // MOBA_BLOCK, MOBA_BLOCK)

    lam_refs = (lq1_ref, lk1_ref, lq2_ref, lk2_ref)
    for n in range(n_tiles):
        pl.when(i == n)(functools.partial(_diff_query_tile, n, q_ref, k_ref, lam_refs,
                                          subln_ref, o_ref, vt_sc, lam_init=lam_init))


_DIFF_HEADS_PER_STEP = 2


def _diff_attention(qkv, lq1, lk1, lq2, lk2, subln_w, batch, seq, heads, col0):
    blk = _QUERY_TILE
    vdim = 2 * HEAD_DIM
    hps = _DIFF_HEADS_PER_STEP if heads % _DIFF_HEADS_PER_STEP == 0 else 1
    width = hps * vdim
    assert seq % blk == 0 and col0 % width == 0
    n_blocks = seq // blk
    c0 = col0 // width
    heads = heads // hps
    vec = pl.BlockSpec((1, HEAD_DIM), lambda b, h, i: (0, 0))
    as_row = lambda t: t.reshape(1, HEAD_DIM).astype(F32)
    return pl.pallas_call(
        functools.partial(_diff_kernel, n_tiles=n_blocks, lam_init=DIFF_LAMBDA_INIT),
        grid=(batch, heads, n_blocks),
        in_specs=[pl.BlockSpec((blk, width), lambda b, h, i: (b * n_blocks + i, c0 + h)),
                  pl.BlockSpec((seq, width), lambda b, h, i: (b, c0 + heads + h)),
                  pl.BlockSpec((seq, width), lambda b, h, i: (b, c0 + 2 * heads + h)),
                  vec, vec, vec, vec,
                  pl.BlockSpec((vdim, 1), lambda b, h, i: (0, 0))],
        out_specs=pl.BlockSpec((blk, width), lambda b, h, i: (b * n_blocks + i, h)),
        out_shape=jax.ShapeDtypeStruct((batch * seq, heads * width), BF16),
        scratch_shapes=[pltpu.VMEM((width, seq), BF16)],
        compiler_params=_cparams("arbitrary", "arbitrary", "arbitrary"),
        name="diff_attention",
    )(qkv, qkv, qkv, as_row(lq1), as_row(lk1), as_row(lq2), as_row(lk2),
      subln_w.reshape(vdim, 1).astype(F32))


def _outproj_kernel(a_ref, b_ref, wa_ref, wb_ref, x_ref, o_ref):
    acc = jnp.dot(a_ref[...], wa_ref[...], preferred_element_type=F32)
    acc += jnp.dot(b_ref[...], wb_ref[...], preferred_element_type=F32)
    o_ref[...] = x_ref[...] + acc


def _outproj(a, b, wa, wb, x):
    m, ka = a.shape
    kb = b.shape[1]
    n = wa.shape[1]
    tm = _pick_tile(m, 1024, SUBLANES_BF16)
    tn = _pick_tile(n, 1024, HEAD_DIM)
    return pl.pallas_call(
        _outproj_kernel,
        grid=(m // tm, n // tn),
        in_specs=[pl.BlockSpec((tm, ka), lambda i, j: (i, 0)),
                  pl.BlockSpec((tm, kb), lambda i, j: (i, 0)),
                  pl.BlockSpec((ka, tn), lambda i, j: (0, j)),
                  pl.BlockSpec((kb, tn), lambda i, j: (0, j)),
                  pl.BlockSpec((tm, tn), lambda i, j: (i, j))],
        out_specs=pl.BlockSpec((tm, tn), lambda i, j: (i, j)),
        out_shape=jax.ShapeDtypeStruct((m, n), F32),
        compiler_params=_cparams("arbitrary", "arbitrary"),
        name="outproj_residual",
    )(a, b, wa, wb, x)


_CONV_CHUNK_ROWS = 256


def _up_kernel(h_ref, wg_ref, wu_ref, pg_ref, pu_ref, o_ref,
               g_sc, u_sc, cg_sc, cu_sc, *, seq_tiles, col_tiles, n_tiles):
    t = pl.program_id(0)
    tm = h_ref.shape[0]
    halo = SUBLANES_F32
    slot = t % 2
    prev = 1 - slot
    tile = jnp.minimum(t, n_tiles - 1)
    i = tile // col_tiles
    j = tile % col_tiles
    seq_start = i % seq_tiles == 0

    @pl.when(t == 0)
    def _():
        g_sc[1] = jnp.zeros(g_sc.shape[1:], F32)
        u_sc[1] = jnp.zeros(u_sc.shape[1:], F32)

    for sc, carry in ((g_sc, cg_sc), (u_sc, cu_sc)):
        @pl.when(seq_start)
        def _(sc=sc):
            sc[slot, 0:halo, :] = jnp.zeros((halo, sc.shape[2]), F32)

        @pl.when(jnp.logical_not(seq_start))
        def _(sc=sc, carry=carry):
            sc[slot, 0:halo, :] = carry[j]

    def conv(sc, p_ref, r0):
        rows = _CONV_CHUNK_ROWS
        out = p_ref[CONV_WIDTH:CONV_WIDTH + 1, :]
        for tap in range(CONV_WIDTH):
            off = r0 + halo - (CONV_WIDTH - 1) + tap
            out = out + sc[prev, off:off + rows, :] * p_ref[tap:tap + 1, :]
        return out

    h = h_ref[...]
    g_sc[slot, halo:halo + tm, :] = jnp.dot(h, wg_ref[...], preferred_element_type=F32)
    cg_sc[j] = g_sc[slot, tm:tm + halo, :]

    for r0 in range(0, tm, _CONV_CHUNK_ROWS):
        g = conv(g_sc, pg_ref, r0)
        u = conv(u_sc, pu_ref, r0)
        o_ref[r0:r0 + _CONV_CHUNK_ROWS, :] = (g * jax.nn.sigmoid(g) * u).astype(o_ref.dtype)

    u_sc[slot, halo:halo + tm, :] = jnp.dot(h, wu_ref[...], preferred_element_type=F32)
    cu_sc[j] = u_sc[slot, tm:tm + halo, :]


def _up_conv_gate(h, wg, wu, pg, pu, seq):
    m, d = h.shape
    n = wg.shape[1]
    tm = _pick_tile(seq, 1024, _CONV_CHUNK_ROWS)
    tn = _pick_tile(n, 512, HEAD_DIM)
    halo = SUBLANES_F32
    col_tiles = n // tn
    n_tiles = (m // tm) * col_tiles
    project = lambda t: jnp.minimum(t, n_tiles - 1)
    finish = lambda t: jnp.maximum(t - 1, 0)
    w_spec = pl.BlockSpec((d, tn), lambda t: (0, project(t) % col_tiles))
    p_spec = pl.BlockSpec((CONV_WIDTH + 1, tn), lambda t: (0, finish(t) % col_tiles))
    return pl.pallas_call(
        functools.partial(_up_kernel, seq_tiles=seq // tm, col_tiles=col_tiles, n_tiles=n_tiles),
        grid=(n_tiles + 1,),
        in_specs=[pl.BlockSpec((tm, d), lambda t: (project(t) // col_tiles, 0)),
                  w_spec, w_spec, p_spec, p_spec],
        out_specs=pl.BlockSpec((tm, tn),
                               lambda t: (finish(t) // col_tiles, finish(t) % col_tiles)),
        out_shape=jax.ShapeDtypeStruct((m, n), BF16),
        scratch_shapes=[pltpu.VMEM((2, tm + halo, tn), F32),
                        pltpu.VMEM((2, tm + halo, tn), F32),
                        pltpu.VMEM((col_tiles, halo, tn), F32),
                        pltpu.VMEM((col_tiles, halo, tn), F32)],
        compiler_params=_cparams("arbitrary"),
        name="up_conv_gate",
    )(h, wg, wu, pg, pu)


def _down_kernel(a_ref, w_ref, x_ref, o_ref):
    @pl.when(pl.program_id(2) == 0)
    def _():
        o_ref[...] = x_ref[...]

    o_ref[...] += jnp.dot(a_ref[...], w_ref[...], preferred_element_type=F32)


def _down_residual(a, w, x):
    m, k = a.shape
    n = w.shape[1]
    tm = _pick_tile(m, 1024, SUBLANES_BF16)
    tn = _pick_tile(n, 1024, HEAD_DIM)
    tk = _pick_tile(k, 2816, HEAD_DIM)
    return pl.pallas_call(
        _down_kernel,
        grid=(m // tm, n // tn, k // tk),
        in_specs=[pl.BlockSpec((tm, tk), lambda i, j, kk: (i, kk)),
                  pl.BlockSpec((tk, tn), lambda i, j, kk: (kk, j)),
                  pl.BlockSpec((tm, tn), lambda i, j, kk: (i, j))],
        out_specs=pl.BlockSpec((tm, tn), lambda i, j, kk: (i, j)),
        out_shape=jax.ShapeDtypeStruct((m, n), F32),
        compiler_params=_cparams("arbitrary", "arbitrary", "arbitrary"),
        name="down_residual",
    )(a, w, x)


_CAST_TILE = 256


def _cast_kernel(x_ref, o_ref, *, n_valid):
    t = pl.program_id(0)

    @pl.when(t < n_valid)
    def _():
        o_ref[...] = x_ref[...].astype(o_ref.dtype)

    @pl.when(t >= n_valid)
    def _():
        o_ref[...] = jnp.zeros(o_ref.shape, o_ref.dtype)


def _cast_pad(w, axis, start, size, padded):
    tile = _CAST_TILE if size % _CAST_TILE == 0 else HEAD_DIM
    assert start % tile == 0 and size % tile == 0 and padded % tile == 0
    first, n_valid = start // tile, size // tile
    other = w.shape[1 - axis]
    block = (tile, other) if axis == 0 else (other, tile)
    at = (lambda t: (t, 0)) if axis == 0 else (lambda t: (0, t))
    out_shape = (padded, other) if axis == 0 else (other, padded)
    return pl.pallas_call(
        functools.partial(_cast_kernel, n_valid=n_valid),
        grid=(padded // tile,),
        in_specs=[pl.BlockSpec(block, lambda t: at(first + jnp.minimum(t, n_valid - 1)))],
        out_specs=pl.BlockSpec(block, at),
        out_shape=jax.ShapeDtypeStruct(out_shape, BF16),
        compiler_params=_cparams("arbitrary"),
        name="cast_pad",
    )(w)


def _pad_cols(t, n):
    return jnp.pad(t, ((0, 0), (0, n - t.shape[1])))


def kernel(x, attn_norm_w, w_in, diff_lambda_q1, diff_lambda_k1, diff_lambda_q2, diff_lambda_k2, diff_subln_w, w_out, ffn_norm_w, w_up, conv_w, conv_b, w_down, final_norm_w):
    batch, seq, d_model = x.shape
    depth = w_in.shape[0]
    assert depth == 1, "lambda_init is folded for a single layer"
    moba_heads = d_model // (2 * HEAD_DIM)
    diff_heads = d_model // (4 * HEAD_DIM)
    moba_width = moba_heads * HEAD_DIM
    d_ff = w_down.shape[1]
    ff_quantum = 1024 if d_ff >= 1024 else HEAD_DIM
    d_ff_pad = -(-d_ff // ff_quantum) * ff_quantum

    xf = x.reshape(batch * seq, d_model)
    tables = _rope_tables(seq, HEAD_DIM ** -0.5 * math.log2(math.e))
    for l in range(depth):
        h = _rmsnorm(xf, attn_norm_w[l], RMS_EPS, BF16)
        qkv = _inproj(h, _cast_pad(w_in[l], 1, 0, w_in.shape[2], w_in.shape[2]), tables, seq,
                      moba_width)
        moba = _moba_attention(qkv, batch, seq, moba_heads)
        diff = _diff_attention(qkv, diff_lambda_q1[l], diff_lambda_k1[l], diff_lambda_q2[l],
                               diff_lambda_k2[l], diff_subln_w[l], batch, seq, diff_heads,
                               3 * moba_width)
        mix_width = w_out.shape[1]
        w_oa = _cast_pad(w_out[l], 0, 0, moba_width, moba_width)
        w_ob = _cast_pad(w_out[l], 0, moba_width, mix_width - moba_width, mix_width - moba_width)
        xf = _outproj(moba, diff, w_oa, w_ob, xf)

        h = _rmsnorm(xf, ffn_norm_w[l], RMS_EPS, BF16)
        conv_params = jnp.concatenate([conv_w[l], conv_b[l][None, :]], axis=0).astype(F32)
        wg = _cast_pad(w_up[l], 1, 0, d_ff, d_ff_pad)
        wu = _cast_pad(w_up[l], 1, d_ff, d_ff, d_ff_pad)
        pg = _pad_cols(conv_params[:, :d_ff], d_ff_pad)
        pu = _pad_cols(conv_params[:, d_ff:], d_ff_pad)
        act = _up_conv_gate(h, wg, wu, pg, pu, seq)
        xf = _down_residual(act, _cast_pad(w_down[l], 0, 0, d_ff, d_ff_pad), xf)
    out = _rmsnorm(xf, final_norm_w, RMS_EPS, x.dtype)
    return out.reshape(batch, seq, d_model)
```

```python
import functools
import math

import jax
import jax.numpy as jnp
from jax import lax
from jax.experimental import pallas as pl
from jax.experimental.pallas import tpu as pltpu

HEAD_DIM = 128
MOBA_BLOCK = 256
MOBA_TOPK = 3
ROPE_THETA = 10000.0
CONV_WIDTH = 3
RMS_EPS = 1e-6
SUBLN_EPS = 1e-5
NEG_INF = -1e30
DIFF_LAMBDA_INIT = 0.8 - 0.6 * math.exp(-0.3 * 0)

V7X_VMEM_LIMIT_BYTES = 56 * 1024 * 1024
SUBLANES_F32 = 8
SUBLANES_BF16 = 16

F32 = jnp.float32
BF16 = jnp.bfloat16
_NT_DIMS = (((1,), (1,)), ((), ()))


def _cparams(*semantics, **extra):
    return pltpu.CompilerParams(dimension_semantics=semantics,
                                vmem_limit_bytes=V7X_VMEM_LIMIT_BYTES, **extra)


def _pick_tile(n, target, quantum):
    t = min(n, target) // quantum * quantum
    while t > quantum and n % t:
        t -= quantum
    assert t >= quantum and n % t == 0, (n, target, quantum)
    return t


def _rmsnorm_kernel(x_ref, w_ref, o_ref, *, eps):
    x = x_ref[...]
    y = x * lax.rsqrt(jnp.mean(x * x, axis=-1, keepdims=True) + eps)
    o_ref[...] = (y * w_ref[...]).astype(o_ref.dtype)


def _rmsnorm(x, w, eps, out_dtype):
    m, d = x.shape
    tm = _pick_tile(m, 256, SUBLANES_BF16)
    return pl.pallas_call(
        functools.partial(_rmsnorm_kernel, eps=eps),
        grid=(m // tm,),
        in_specs=[pl.BlockSpec((tm, d), lambda i: (i, 0)),
                  pl.BlockSpec((1, d), lambda i: (0, 0))],
        out_specs=pl.BlockSpec((tm, d), lambda i: (i, 0)),
        out_shape=jax.ShapeDtypeStruct((m, d), out_dtype),
        compiler_params=_cparams("arbitrary"),
        name="rmsnorm",
    )(x, w.reshape(1, d).astype(F32))


def _on_parity(t, step):
    pl.when(t % 2 == 0)(functools.partial(step, 0, 1))
    pl.when(t % 2 == 1)(functools.partial(step, 1, 0))


def _inproj_kernel(h_ref, w_ref, cs_ref, o_ref, acc0_sc, acc1_sc, *, tiles_per_group, col_tiles):
    t = pl.program_id(0)
    accs = (acc0_sc, acc1_sc)
    finish_group = ((jnp.maximum(t - 1, 0) % col_tiles) // tiles_per_group) % 3

    @pl.when(t == 0)
    def _():
        acc1_sc[...] = jnp.zeros(acc1_sc.shape, F32)

    def step(rotate, project, finish):
        accs[project][...] = jnp.dot(h_ref[...], w_ref[...], preferred_element_type=F32)
        acc = accs[finish]
        if not rotate:
            o_ref[...] = acc[...].astype(o_ref.dtype)
            return
        cos = cs_ref[0, 0]
        sin = cs_ref[0, 1]
        for hh in range(acc.shape[1] // HEAD_DIM):
            cols = slice(hh * HEAD_DIM, (hh + 1) * HEAD_DIM)
            a = acc[:, cols]
            rot = pltpu.roll(a, HEAD_DIM // 2, 1)
            o_ref[:, cols] = (a * cos + rot * sin).astype(o_ref.dtype)

    pl.when(finish_group != 2)(lambda: _on_parity(t, functools.partial(step, True)))
    pl.when(finish_group == 2)(lambda: _on_parity(t, functools.partial(step, False)))


def _inproj(h, w, tables, seq, group_width):
    m, d = h.shape
    n = w.shape[1]
    tm = _pick_tile(seq, 1024, SUBLANES_BF16)
    tn = _pick_tile(group_width, 1024, HEAD_DIM)
    tiles_per_group = group_width // tn
    seq_tiles = seq // tm
    col_tiles = n // tn
    n_tiles = (m // tm) * col_tiles
    project = lambda t: jnp.minimum(t, n_tiles - 1)
    finish = lambda t: jnp.maximum(t - 1, 0)

    def table_map(t):
        i, j = finish(t) // col_tiles, finish(t) % col_tiles
        return (jnp.minimum((j // tiles_per_group) % 3, 1), 0, i % seq_tiles, 0)

    return pl.pallas_call(
        functools.partial(_inproj_kernel, tiles_per_group=tiles_per_group, col_tiles=col_tiles),
        grid=(n_tiles + 1,),
        in_specs=[pl.BlockSpec((tm, d), lambda t: (project(t) // col_tiles, 0)),
                  pl.BlockSpec((d, tn), lambda t: (0, project(t) % col_tiles)),
                  pl.BlockSpec((1, 2, tm, HEAD_DIM), table_map)],
        out_specs=pl.BlockSpec((tm, tn), lambda t: (finish(t) // col_tiles, finish(t) % col_tiles)),
        out_shape=jax.ShapeDtypeStruct((m, n), BF16),
        scratch_shapes=[pltpu.VMEM((tm, tn), F32), pltpu.VMEM((tm, tn), F32)],
        compiler_params=_cparams("arbitrary"),
        name="inproj_rope",
    )(h, w, tables)


def _rope_tables(seq, q_scale):
    inv = 1.0 / (ROPE_THETA ** (jnp.arange(0, HEAD_DIM, 2, dtype=F32) / HEAD_DIM))
    ang = jnp.arange(seq, dtype=F32)[:, None] * inv[None, :]
    cos, sin = jnp.cos(ang), jnp.sin(ang)
    cos_full = jnp.concatenate([cos, cos], axis=-1)
    sin_signed = jnp.concatenate([-sin, sin], axis=-1)
    k_tab = jnp.stack([cos_full, sin_signed])
    return jnp.stack([k_tab * q_scale, k_tab])


def _own_block_scores(k_own, q):
    blk = k_own.shape[0]
    s_t = lax.dot_general(k_own, q, _NT_DIMS, preferred_element_type=F32)
    k_pos = lax.broadcasted_iota(jnp.int32, (blk, blk), 0)
    q_pos = lax.broadcasted_iota(jnp.int32, (blk, blk), 1)
    return jnp.where(k_pos <= q_pos, s_t, NEG_INF)


def _col_max(parts):
    m = jnp.max(parts[0], axis=0, keepdims=True)
    for part in parts[1:]:
        m = jnp.maximum(m, jnp.max(part, axis=0, keepdims=True))
    return m


def _col_sum(parts):
    l = jnp.sum(parts[0], axis=0, keepdims=True)
    for part in parts[1:]:
        l = l + jnp.sum(part, axis=0, keepdims=True)
    return l


def _transpose_blocks(v_ref, vt_sc, n_blocks, blk):
    for n in range(n_blocks):
        rows = slice(n * blk, (n + 1) * blk)
        vt_sc[:, rows] = v_ref[rows, :].astype(F32).T.astype(BF16)


def _moba_gate_bias(kmean_hi_lo, q, a, topk):
    blk = MOBA_BLOCK
    nbp = kmean_hi_lo.shape[0] // 2
    g2 = lax.dot_general(kmean_hi_lo, q, _NT_DIMS, preferred_element_type=F32)
    g = g2[0:nbp, :] + g2[nbp:2 * nbp, :]
    row = lax.broadcasted_iota(jnp.int32, g.shape, 0)
    col = lax.broadcasted_iota(jnp.int32, g.shape, 1)
    valid = row < jnp.where(col >= blk, a + 1, a)
    biases = []
    for m in range(a + 1):
        gm = g[m:m + 1, :]
        beats = valid & ((g > gm) | ((g == gm) & (row < m)))
        rank = jnp.sum(beats.astype(F32), axis=0, keepdims=True)
        biases.append(jnp.where(rank < topk, 0.0, NEG_INF))
    return biases


def _moba_head(tile, q, k_ref, cols, vt, kmean_hi_lo, topk, o_ref):
    blk = MOBA_BLOCK
    a = 2 * tile
    n_keys = (a + 2) * blk
    win = _own_block_scores(k_ref[a * blk:n_keys, cols], q)
    win_a, win_b = win[0:blk, :], win[blk:2 * blk, :]
    if a + 1 > topk:
        biases = _moba_gate_bias(kmean_hi_lo, q, a, topk)
        second = lax.broadcasted_iota(jnp.int32, (1, 2 * blk), 1) >= blk
        bias_a = jnp.where(second, biases[a], 0.0)
    else:
        biases, bias_a = [0.0] * a, 0.0
    past = []
    if a > 0:
        s_past = lax.dot_general(k_ref[0:a * blk, cols], q, _NT_DIMS, preferred_element_type=F32)
        past = [s_past[j * blk:(j + 1) * blk, :] for j in range(a)]
    yield
    m = jnp.maximum(jnp.max(win_a, axis=0, keepdims=True) + bias_a,
                    jnp.max(win_b, axis=0, keepdims=True))
    for part, bias in zip(past, biases):
        m = jnp.maximum(m, jnp.max(part, axis=0, keepdims=True) + bias)
    yield
    probs = [jnp.exp2(part + (bias - m)) for part, bias in zip(past, biases)]
    probs += [jnp.exp2(win_a + (bias_a - m)), jnp.exp2(win_b - m)]
    l = _col_sum(probs)
    p = jnp.concatenate(probs, axis=0).astype(BF16)
    yield
    acc = jnp.dot(vt[:, 0:n_keys], p, preferred_element_type=F32)
    yield
    o_ref[:, cols] = (acc / l).T.astype(o_ref.dtype)


_ALTERNATE_BELOW_TILE = 6


def _run_phases(chains, alternate):
    if not alternate:
        for chain in chains:
            for _ in chain:
                pass
        return
    while chains:
        chains = [chain for chain in chains if next(chain, True) is None]


def _moba_query_tile(tile, q_ref, k_ref, o_ref, vt_sc, kmean_sc, *, topk):
    heads = []
    for hh in range(q_ref.shape[1] // HEAD_DIM):
        cols = slice(hh * HEAD_DIM, (hh + 1) * HEAD_DIM)
        heads.append(_moba_head(tile, q_ref[:, cols], k_ref, cols, vt_sc.at[cols, :], kmean_sc[hh],
                                topk, o_ref))
    _run_phases(heads, alternate=tile < _ALTERNATE_BELOW_TILE)


def _moba_kernel(q_ref, k_ref, v_ref, o_ref, vt_sc, kmean_sc, *, n_blocks, topk):
    i = pl.program_id(2)
    blk = MOBA_BLOCK
    nbp = kmean_sc.shape[1] // 2
    seq = k_ref.shape[0]

    @pl.when(i == 0)
    def _():
        _transpose_blocks(v_ref, vt_sc, n_blocks, blk)
        row = lax.broadcasted_iota(jnp.int32, (nbp, seq), 0)
        col = lax.broadcasted_iota(jnp.int32, (nbp, seq), 1)
        lo_edge = row * blk
        avg = jnp.where((col >= lo_edge) & (col < lo_edge + blk), 1.0 / blk, 0.0).astype(BF16)
        kmean = jnp.dot(avg, k_ref[...], preferred_element_type=F32)
        hi = kmean.astype(BF16)
        lo = (kmean - hi.astype(F32)).astype(BF16)
        for hh in range(kmean_sc.shape[0]):
            cols = slice(hh * HEAD_DIM, (hh + 1) * HEAD_DIM)
            kmean_sc[hh, 0:nbp, :] = hi[:, cols]
            kmean_sc[hh, nbp:2 * nbp, :] = lo[:, cols]

    for tile in range(n_blocks // 2):
        pl.when(i == tile)(functools.partial(_moba_query_tile, tile, q_ref, k_ref, o_ref, vt_sc,
                                             kmean_sc, topk=topk))


_MOBA_HEADS_PER_STEP = 1
_QUERY_TILE = 2 * MOBA_BLOCK


def _moba_attention(qkv, batch, seq, heads):
    blk = MOBA_BLOCK
    hps = _MOBA_HEADS_PER_STEP
    assert seq % _QUERY_TILE == 0 and heads % hps == 0
    n_blocks = seq // blk
    n_tiles = seq // _QUERY_TILE
    nbp = -(-n_blocks // SUBLANES_BF16) * SUBLANES_BF16
    topk = min(MOBA_TOPK, n_blocks)
    width = hps * HEAD_DIM
    groups = heads // hps
    return pl.pallas_call(
        functools.partial(_moba_kernel, n_blocks=n_blocks, topk=topk),
        grid=(batch, groups, n_tiles),
        in_specs=[pl.BlockSpec((_QUERY_TILE, width), lambda b, h, i: (b * n_tiles + i, h)),
                  pl.BlockSpec((seq, width), lambda b, h, i: (b, groups + h)),
                  pl.BlockSpec((seq, width), lambda b, h, i: (b, 2 * groups + h))],
        out_specs=pl.BlockSpec((_QUERY_TILE, width), lambda b, h, i: (b * n_tiles + i, h)),
        out_shape=jax.ShapeDtypeStruct((batch * seq, heads * HEAD_DIM), BF16),
        scratch_shapes=[pltpu.VMEM((width, seq), BF16),
                        pltpu.VMEM((hps, 2 * nbp, HEAD_DIM), BF16)],
        compiler_params=_cparams("arbitrary", "arbitrary", "arbitrary"),
        name="moba_attention",
    )(qkv, qkv, qkv)


def _diff_query_tile(n, q_ref, k_ref, lam_refs, subln_ref, o_ref, vt_sc, *, lam_init):
    blk = q_ref.shape[0]
    lq1_ref, lk1_ref, lq2_ref, lk2_ref = lam_refs
    lam = (jnp.exp(jnp.sum(lq1_ref[...] * lk1_ref[...], axis=-1, keepdims=True))
           - jnp.exp(jnp.sum(lq2_ref[...] * lk2_ref[...], axis=-1, keepdims=True))
           + lam_init)
    probs, sums = [], []
    for c in range(2):
        cols = slice(c * HEAD_DIM, (c + 1) * HEAD_DIM)
        q = q_ref[:, cols]
        parts = []
        if n > 0:
            parts.append(lax.dot_general(k_ref[0:n * blk, cols], q, _NT_DIMS,
                                         preferred_element_type=F32))
        parts.append(_own_block_scores(k_ref[n * blk:(n + 1) * blk, cols], q))
        m = _col_max(parts)
        probs.append([jnp.exp2(part - m) for part in parts])
        sums.append(_col_sum(probs[c]))
    ratio = -lam * sums[0] / sums[1]
    a_parts = [p0 + ratio * p1 for p0, p1 in zip(*probs)]
    a = a_parts[0] if n == 0 else jnp.concatenate(a_parts, axis=0)
    d = jnp.dot(vt_sc[:, 0:(n + 1) * blk], a.astype(BF16), preferred_element_type=F32)
    d = d / sums[0]
    y = d * lax.rsqrt(jnp.mean(d * d, axis=0, keepdims=True) + SUBLN_EPS)
    y = (y * subln_ref[...]) * (1.0 - lam_init)
    o_ref[...] = y.T.astype(o_ref.dtype)


def _diff_kernel(q_ref, k_ref, v_ref, lq1_ref, lk1_ref, lq2_ref, lk2_ref, subln_ref, o_ref,
                 vt_sc, *, n_tiles, lam_init):
    i = pl.program_id(2)

    @pl.when(i == 0)
    def _():
        _transpose_blocks(v_ref, vt_sc, v_ref.shape[0] // MOBA_BLOCK, MOBA_BLOCK)

    lam_refs = (lq1_ref, lk1_ref, lq2_ref, lk2_ref)
    for n in range(n_tiles):
        pl.when(i == n)(functools.partial(_diff_query_tile, n, q_ref, k_ref, lam_refs,
                                          subln_ref, o_ref, vt_sc, lam_init=lam_init))


def _diff_attention(qkv, lq1, lk1, lq2, lk2, subln_w, batch, seq, heads, col0):
    blk = _QUERY_TILE
    vdim = 2 * HEAD_DIM
    assert seq % blk == 0 and col0 % vdim == 0
    n_blocks = seq // blk
    c0 = col0 // vdim
    vec = pl.BlockSpec((1, HEAD_DIM), lambda b, h, i: (0, 0))
    as_row = lambda t: t.reshape(1, HEAD_DIM).astype(F32)
    return pl.pallas_call(
        functools.partial(_diff_kernel, n_tiles=n_blocks, lam_init=DIFF_LAMBDA_INIT),
        grid=(batch, heads, n_blocks),
        in_specs=[pl.BlockSpec((blk, vdim), lambda b, h, i: (b * n_blocks + i, c0 + h)),
                  pl.BlockSpec((seq, vdim), lambda b, h, i: (b, c0 + heads + h)),
                  pl.BlockSpec((seq, vdim), lambda b, h, i: (b, c0 + 2 * heads + h)),
                  vec, vec, vec, vec,
                  pl.BlockSpec((vdim, 1), lambda b, h, i: (0, 0))],
        out_specs=pl.BlockSpec((blk, vdim), lambda b, h, i: (b * n_blocks + i, h)),
        out_shape=jax.ShapeDtypeStruct((batch * seq, heads * vdim), BF16),
        scratch_shapes=[pltpu.VMEM((vdim, seq), BF16)],
        compiler_params=_cparams("arbitrary", "arbitrary", "arbitrary"),
        name="diff_attention",
    )(qkv, qkv, qkv, as_row(lq1), as_row(lk1), as_row(lq2), as_row(lk2),
      subln_w.reshape(vdim, 1).astype(F32))


def _outproj_kernel(a_ref, b_ref, wa_ref, wb_ref, x_ref, o_ref):
    acc = jnp.dot(a_ref[...], wa_ref[...], preferred_element_type=F32)
    acc += jnp.dot(b_ref[...], wb_ref[...], preferred_element_type=F32)
    o_ref[...] = x_ref[...] + acc


def _outproj(a, b, wa, wb, x):
    m, ka = a.shape
    kb = b.shape[1]
    n = wa.shape[1]
    tm = _pick_tile(m, 1024, SUBLANES_BF16)
    tn = _pick_tile(n, 1024, HEAD_DIM)
    return pl.pallas_call(
        _outproj_kernel,
        grid=(m // tm, n // tn),
        in_specs=[pl.BlockSpec((tm, ka), lambda i, j: (i, 0)),
                  pl.BlockSpec((tm, kb), lambda i, j: (i, 0)),
                  pl.BlockSpec((ka, tn), lambda i, j: (0, j)),
                  pl.BlockSpec((kb, tn), lambda i, j: (0, j)),
                  pl.BlockSpec((tm, tn), lambda i, j: (i, j))],
        out_specs=pl.BlockSpec((tm, tn), lambda i, j: (i, j)),
        out_shape=jax.ShapeDtypeStruct((m, n), F32),
        compiler_params=_cparams("arbitrary", "arbitrary"),
        name="outproj_residual",
    )(a, b, wa, wb, x)


_CONV_CHUNK_ROWS = 256


def _up_kernel(h_ref, wg_ref, wu_ref, pg_ref, pu_ref, o_ref,
               g_sc, u_sc, cg_sc, cu_sc, *, seq_tiles, col_tiles, n_tiles):
    t = pl.program_id(0)
    tm = h_ref.shape[0]
    halo = SUBLANES_F32
    slot = t % 2
    prev = 1 - slot
    tile = jnp.minimum(t, n_tiles - 1)
    i = tile // col_tiles
    j = tile % col_tiles
    seq_start = i % seq_tiles == 0

    @pl.when(t == 0)
    def _():
        g_sc[1] = jnp.zeros(g_sc.shape[1:], F32)
        u_sc[1] = jnp.zeros(u_sc.shape[1:], F32)

    for sc, carry in ((g_sc, cg_sc), (u_sc, cu_sc)):
        @pl.when(seq_start)
        def _(sc=sc):
            sc[slot, 0:halo, :] = jnp.zeros((halo, sc.shape[2]), F32)

        @pl.when(jnp.logical_not(seq_start))
        def _(sc=sc, carry=carry):
            sc[slot, 0:halo, :] = carry[j]

    def conv(sc, p_ref, r0):
        rows = _CONV_CHUNK_ROWS
        out = p_ref[CONV_WIDTH:CONV_WIDTH + 1, :]
        for tap in range(CONV_WIDTH):
            off = r0 + halo - (CONV_WIDTH - 1) + tap
            out = out + sc[prev, off:off + rows, :] * p_ref[tap:tap + 1, :]
        return out

    h = h_ref[...]
    g_sc[slot, halo:halo + tm, :] = jnp.dot(h, wg_ref[...], preferred_element_type=F32)
    cg_sc[j] = g_sc[slot, tm:tm + halo, :]

    for r0 in range(0, tm, _CONV_CHUNK_ROWS):
        g = conv(g_sc, pg_ref, r0)
        u = conv(u_sc, pu_ref, r0)
        o_ref[r0:r0 + _CONV_CHUNK_ROWS, :] = (g * jax.nn.sigmoid(g) * u).astype(o_ref.dtype)

    u_sc[slot, halo:halo + tm, :] = jnp.dot(h, wu_ref[...], preferred_element_type=F32)
    cu_sc[j] = u_sc[slot, tm:tm + halo, :]


def _up_conv_gate(h, wg, wu, pg, pu, seq):
    m, d = h.shape
    n = wg.shape[1]
    tm = _pick_tile(seq, 1024, _CONV_CHUNK_ROWS)
    tn = _pick_tile(n, 512, HEAD_DIM)
    halo = SUBLANES_F32
    col_tiles = n // tn
    n_tiles = (m // tm) * col_tiles
    project = lambda t: jnp.minimum(t, n_tiles - 1)
    finish = lambda t: jnp.maximum(t - 1, 0)
    w_spec = pl.BlockSpec((d, tn), lambda t: (0, project(t) % col_tiles))
    p_spec = pl.BlockSpec((CONV_WIDTH + 1, tn), lambda t: (0, finish(t) % col_tiles))
    return pl.pallas_call(
        functools.partial(_up_kernel, seq_tiles=seq // tm, col_tiles=col_tiles, n_tiles=n_tiles),
        grid=(n_tiles + 1,),
        in_specs=[pl.BlockSpec((tm, d), lambda t: (project(t) // col_tiles, 0)),
                  w_spec, w_spec, p_spec, p_spec],
        out_specs=pl.BlockSpec((tm, tn),
                               lambda t: (finish(t) // col_tiles, finish(t) % col_tiles)),
        out_shape=jax.ShapeDtypeStruct((m, n), BF16),
        scratch_shapes=[pltpu.VMEM((2, tm + halo, tn), F32),
                        pltpu.VMEM((2, tm + halo, tn), F32),
                        pltpu.VMEM((col_tiles, halo, tn), F32),
                        pltpu.VMEM((col_tiles, halo, tn), F32)],
        compiler_params=_cparams("arbitrary"),
        name="up_conv_gate",
    )(h, wg, wu, pg, pu)


def _down_kernel(a_ref, w_ref, x_ref, o_ref):
    @pl.when(pl.program_id(2) == 0)
    def _():
        o_ref[...] = x_ref[...]

    o_ref[...] += jnp.dot(a_ref[...], w_ref[...], preferred_element_type=F32)


def _down_residual(a, w, x):
    m, k = a.shape
    n = w.shape[1]
    tm = _pick_tile(m, 1024, SUBLANES_BF16)
    tn = _pick_tile(n, 1024, HEAD_DIM)
    tk = _pick_tile(k, 2816, HEAD_DIM)
    return pl.pallas_call(
        _down_kernel,
        grid=(m // tm, n // tn, k // tk),
        in_specs=[pl.BlockSpec((tm, tk), lambda i, j, kk: (i, kk)),
                  pl.BlockSpec((tk, tn), lambda i, j, kk: (kk, j)),
                  pl.BlockSpec((tm, tn), lambda i, j, kk: (i, j))],
        out_specs=pl.BlockSpec((tm, tn), lambda i, j, kk: (i, j)),
        out_shape=jax.ShapeDtypeStruct((m, n), F32),
        compiler_params=_cparams("arbitrary", "arbitrary", "arbitrary"),
        name="down_residual",
    )(a, w, x)


_CAST_TILE = 256


def _cast_kernel(x_ref, o_ref, *, n_valid):
    t = pl.program_id(0)

    @pl.when(t < n_valid)
    def _():
        o_ref[...] = x_ref[...].astype(o_ref.dtype)

    @pl.when(t >= n_valid)
    def _():
        o_ref[...] = jnp.zeros(o_ref.shape, o_ref.dtype)


def _cast_pad(w, axis, start, size, padded):
    tile = _CAST_TILE if size % _CAST_TILE == 0 else HEAD_DIM
    assert start % tile == 0 and size % tile == 0 and padded % tile == 0
    first, n_valid = start // tile, size // tile
    other = w.shape[1 - axis]
    block = (tile, other) if axis == 0 else (other, tile)
    at = (lambda t: (t, 0)) if axis == 0 else (lambda t: (0, t))
    out_shape = (padded, other) if axis == 0 else (other, padded)
    return pl.pallas_call(
        functools.partial(_cast_kernel, n_valid=n_valid),
        grid=(padded // tile,),
        in_specs=[pl.BlockSpec(block, lambda t: at(first + jnp.minimum(t, n_valid - 1)))],
        out_specs=pl.BlockSpec(block, at),
        out_shape=jax.ShapeDtypeStruct(out_shape, BF16),
        compiler_params=_cparams("arbitrary"),
        name="cast_pad",
    )(w)


def _pad_cols(t, n):
    return jnp.pad(t, ((0, 0), (0, n - t.shape[1])))


def kernel(x, attn_norm_w, w_in, diff_lambda_q1, diff_lambda_k1, diff_lambda_q2, diff_lambda_k2, diff_subln_w, w_out, ffn_norm_w, w_up, conv_w, conv_b, w_down, final_norm_w):
    batch, seq, d_model = x.shape
    depth = w_in.shape[0]
    assert depth == 1, "lambda_init is folded for a single layer"
    moba_heads = d_model // (2 * HEAD_DIM)
    diff_heads = d_model // (4 * HEAD_DIM)
    moba_width = moba_heads * HEAD_DIM
    d_ff = w_down.shape[1]
    ff_quantum = 1024 if d_ff >= 1024 else HEAD_DIM
    d_ff_pad = -(-d_ff // ff_quantum) * ff_quantum

    xf = x.reshape(batch * seq, d_model)
    tables = _rope_tables(seq, HEAD_DIM ** -0.5 * math.log2(math.e))
    for l in range(depth):
        h = _rmsnorm(xf, attn_norm_w[l], RMS_EPS, BF16)
        qkv = _inproj(h, _cast_pad(w_in[l], 1, 0, w_in.shape[2], w_in.shape[2]), tables, seq,
                      moba_width)
        moba = _moba_attention(qkv, batch, seq, moba_heads)
        diff = _diff_attention(qkv, diff_lambda_q1[l], diff_lambda_k1[l], diff_lambda_q2[l],
                               diff_lambda_k2[l], diff_subln_w[l], batch, seq, diff_heads,
                               3 * moba_width)
        mix_width = w_out.shape[1]
        w_oa = _cast_pad(w_out[l], 0, 0, moba_width, moba_width)
        w_ob = _cast_pad(w_out[l], 0, moba_width, mix_width - moba_width, mix_width - moba_width)
        xf = _outproj(moba, diff, w_oa, w_ob, xf)

        h = _rmsnorm(xf, ffn_norm_w[l], RMS_EPS, BF16)
        conv_params = jnp.concatenate([conv_w[l], conv_b[l][None, :]], axis=0).astype(F32)
        wg = _cast_pad(w_up[l], 1, 0, d_ff, d_ff_pad)
        wu = _cast_pad(w_up[l], 1, d_ff, d_ff, d_ff_pad)
        pg = _pad_cols(conv_params[:, :d_ff], d_ff_pad)
        pu = _pad_cols(conv_params[:, d_ff:], d_ff_pad)
        act = _up_conv_gate(h, wg, wu, pg, pu, seq)
        xf = _down_residual(act, _cast_pad(w_down[l], 0, 0, d_ff, d_ff_pad), xf)
    out = _rmsnorm(xf, final_norm_w, RMS_EPS, x.dtype)
    return out.reshape(batch, seq, d_model)
```
